```python
import jax
import jax.numpy as jnp
from jax import lax
import numpy as np

D_MODEL = 1024
BATCH = 4
SEQ = 4096
DEPTH = 1
DEC_BATCH = 2
DEC_SEQ = 16384
PAST_LEN = 128

HEAD_DIM = 64
HEADS_PER_GROUP = 4
DILATION_GROUPS = ((128, 1), (512, 4), (2048, 16))
N_ATTN_GROUPS = len(DILATION_GROUPS)
N_ATTN_HEADS = N_ATTN_GROUPS * HEADS_PER_GROUP
ATTN_WIDTH = N_ATTN_HEADS * HEAD_DIM
CONV_WIDTH = D_MODEL - ATTN_WIDTH
IN_WIDTH = 3 * ATTN_WIDTH + 2 * CONV_WIDTH
CONV_KERNEL = 31
ROPE_THETA = 10000.0
N_EXPERT_GROUPS = 4
EXPERTS_PER_GROUP = 8
N_EXPERTS = N_EXPERT_GROUPS * EXPERTS_PER_GROUP
TOP_K_INNER = 2
EXPERT_HIDDEN = D_MODEL // 2
MOE_BLOCK = 256
NORM_EPS = 1e-6
NEG_INF = -1e30

kernel_name = 'hybrid_dilated_attn_conformer_hmoe_encoder'


def rms_norm(x, g):
    xf = x.astype(jnp.float32)
    y = xf * lax.rsqrt(jnp.mean(xf * xf, axis=-1, keepdims=True) + NORM_EPS)
    return (y * g.astype(jnp.float32)).astype(x.dtype)


def layer_norm(x, g, b):
    xf = x.astype(jnp.float32)
    mu = jnp.mean(xf, axis=-1, keepdims=True)
    var = jnp.mean(jnp.square(xf - mu), axis=-1, keepdims=True)
    y = (xf - mu) * lax.rsqrt(var + NORM_EPS)
    return (y * g.astype(jnp.float32) + b.astype(jnp.float32)).astype(x.dtype)


def rope(x):
    S = x.shape[1]
    half = HEAD_DIM // 2
    inv_freq = ROPE_THETA ** (-jnp.arange(half, dtype=jnp.float32) / half)
    ang = jnp.arange(S, dtype=jnp.float32)[:, None] * inv_freq[None, :]
    cos = jnp.cos(ang)[None, :, None, :]
    sin = jnp.sin(ang)[None, :, None, :]
    xf = x.astype(jnp.float32)
    x1, x2 = xf[..., :half], xf[..., half:]
    return jnp.concatenate([x1 * cos - x2 * sin, x2 * cos + x1 * sin], axis=-1).astype(x.dtype)


def dilated_window_attention(q, k, v, dilation, half):
    B, S, H, Dh = q.shape
    L = S // dilation
    nb = -(-L // half)
    Lp = nb * half

    def to_blocks(t):
        t = t.reshape(B, L, dilation, H, Dh).transpose(0, 2, 1, 3, 4)
        t = jnp.pad(t, ((0, 0), (0, 0), (0, Lp - L), (0, 0), (0, 0)))
        return t.reshape(B, dilation, nb, half, H, Dh)

    def neighbours(t):
        t = jnp.pad(t, ((0, 0), (0, 0), (1, 1), (0, 0), (0, 0), (0, 0)))
        return jnp.concatenate([t[:, :, :-2], t[:, :, 1:-1], t[:, :, 2:]], axis=3)

    qb = to_blocks(q)
    kn = neighbours(to_blocks(k))
    vn = neighbours(to_blocks(v))
    scores = jnp.einsum('brnqhd,brnkhd->brnhqk', qb, kn,
                        preferred_element_type=jnp.float32) * (Dh ** -0.5)
    blk = jnp.arange(nb)[:, None]
    qpos = blk * half + jnp.arange(half)[None, :]
    kpos = (blk - 1) * half + jnp.arange(3 * half)[None, :]
    dist = kpos[:, None, :] - qpos[:, :, None]
    valid = (jnp.abs(dist) <= half) & (kpos[:, None, :] >= 0) & (kpos[:, None, :] < L)
    scores = jnp.where(valid[None, None, :, None, :, :], scores, NEG_INF)
    m = jnp.max(scores, axis=-1, keepdims=True)
    p = jnp.exp(scores - m)
    den = jnp.sum(p, axis=-1, keepdims=True)
    out = jnp.einsum('brnhqk,brnkhd->brnhqd', p, vn.astype(jnp.float32)) / den
    lse = (m + jnp.log(den))[..., 0]
    out = out.transpose(0, 1, 2, 4, 3, 5).reshape(B, dilation, Lp, H, Dh)[:, :, :L]
    out = out.transpose(0, 2, 1, 3, 4).reshape(B, S, H, Dh)
    lse = lse.transpose(0, 1, 2, 4, 3).reshape(B, dilation, Lp, H)[:, :, :L]
    lse = lse.transpose(0, 2, 1, 3).reshape(B, S, H)
    return out.astype(q.dtype), lse


def conformer_conv(c, dw_w, dw_b, ln_g, ln_b):
    a, gate = c[..., :CONV_WIDTH], c[..., CONV_WIDTH:]
    h = a * jax.nn.sigmoid(gate)
    h = lax.conv_general_dilated(
        h, dw_w[:, None, :], window_strides=(1,),
        padding=((CONV_KERNEL // 2, CONV_KERNEL // 2),),
        dimension_numbers=('NWC', 'WIO', 'NWC'),
        feature_group_count=CONV_WIDTH) + dw_b
    h = layer_norm(h, ln_g, ln_b)
    return jax.nn.silu(h)


def hierarchical_moe(x, w_rg, b_rg, w_re, b_re, w_gate, w_up, w_down):
    T, D = x.shape
    xf = x.astype(jnp.float32)
    g_logits = xf @ w_rg.astype(jnp.float32) + b_rg.astype(jnp.float32)
    g_prob = jax.nn.softmax(g_logits, axis=-1)
    grp = jnp.argmax(g_logits, axis=-1)
    p_grp = jnp.take_along_axis(g_prob, grp[:, None], axis=1)[:, 0]
    e_logits = (xf @ w_re.astype(jnp.float32) + b_re.astype(jnp.float32)).reshape(
        T, N_EXPERT_GROUPS, EXPERTS_PER_GROUP)
    e_in = jnp.take_along_axis(e_logits, grp[:, None, None], axis=1)[:, 0]
    top_v, top_i = lax.top_k(e_in, TOP_K_INNER)
    top_w = jax.nn.softmax(top_v, axis=-1) * p_grp[:, None]
    A = T * TOP_K_INNER
    eid = (grp[:, None] * EXPERTS_PER_GROUP + top_i).reshape(A)
    tok = jnp.repeat(jnp.arange(T, dtype=jnp.int32), TOP_K_INNER)
    gate = top_w.reshape(A)
    order = jnp.argsort(eid)
    s_eid, s_tok, s_gate = eid[order], tok[order], gate[order]
    counts = jnp.bincount(eid, length=N_EXPERTS)
    start = jnp.cumsum(counts) - counts
    pcounts = (counts + MOE_BLOCK - 1) // MOE_BLOCK * MOE_BLOCK
    pend = jnp.cumsum(pcounts)
    pstart = pend - pcounts
    dest = pstart[s_eid] + jnp.arange(A) - start[s_eid]
    n_blocks = -(-A // MOE_BLOCK) + N_EXPERTS
    rows = n_blocks * MOE_BLOCK
    row_tok = jnp.zeros((rows,), jnp.int32).at[dest].set(s_tok)
    row_gate = jnp.zeros((rows,), jnp.float32).at[dest].set(s_gate)
    block_expert = jnp.minimum(
        jnp.searchsorted(pend, jnp.arange(n_blocks) * MOE_BLOCK, side='right'),
        N_EXPERTS - 1)

    def expert_block(args):
        e, toks, gts = args
        xb = x[toks]
        h = jax.nn.silu(xb @ w_gate[e]) * (xb @ w_up[e])
        return (h @ w_down[e]).astype(jnp.float32) * gts[:, None]

    yb = lax.map(expert_block, (block_expert,
                                row_tok.reshape(n_blocks, MOE_BLOCK),
                                row_gate.reshape(n_blocks, MOE_BLOCK)))
    out = jnp.zeros((T, D), jnp.float32).at[row_tok].add(yb.reshape(rows, D))
    return out.astype(x.dtype)


def encoder_layer(x, norm1_g, w_in, conv_dw_w, conv_dw_b, conv_ln_g, conv_ln_b, w_out,
                  norm2_g, router_w_group, router_b_group, router_w_expert, router_b_expert,
                  expert_w_gate, expert_w_up, expert_w_down):
    B, S, D = x.shape
    u = rms_norm(x, norm1_g)
    proj = u @ w_in
    q = rope(proj[..., :ATTN_WIDTH].reshape(B, S, N_ATTN_HEADS, HEAD_DIM))
    k = rope(proj[..., ATTN_WIDTH:2 * ATTN_WIDTH].reshape(B, S, N_ATTN_HEADS, HEAD_DIM))
    v = proj[..., 2 * ATTN_WIDTH:3 * ATTN_WIDTH].reshape(B, S, N_ATTN_HEADS, HEAD_DIM)
    c = proj[..., 3 * ATTN_WIDTH:]
    outs, lses = [], []
    for g, (window, dilation) in enumerate(DILATION_GROUPS):
        hs = slice(g * HEADS_PER_GROUP, (g + 1) * HEADS_PER_GROUP)
        o, l = dilated_window_attention(q[:, :, hs], k[:, :, hs], v[:, :, hs],
                                        dilation, window // (2 * dilation))
        outs.append(o)
        lses.append(l)
    alpha = jax.nn.softmax(jnp.stack(lses, axis=0), axis=0)
    attn = jnp.concatenate(
        [outs[g] * alpha[g][..., None].astype(outs[g].dtype) for g in range(N_ATTN_GROUPS)],
        axis=2).reshape(B, S, ATTN_WIDTH)
    conv = conformer_conv(c, conv_dw_w, conv_dw_b, conv_ln_g, conv_ln_b)
    h = x + jnp.concatenate([attn, conv], axis=-1) @ w_out
    moe = hierarchical_moe(rms_norm(h, norm2_g).reshape(B * S, D), router_w_group,
                           router_b_group, router_w_expert, router_b_expert,
                           expert_w_gate, expert_w_up, expert_w_down)
    return h + moe.reshape(B, S, D)


def setup_inputs(seed: int = 0) -> dict:
    key = jax.random.key(seed)
    ks = jax.random.split(key, 18)

    def nrm(k, shape, scale):
        return jax.random.normal(k, shape, jnp.float32) * scale

    return {
        'x_prompt': nrm(ks[0], (BATCH, SEQ, D_MODEL), 1.0),
        'x_sample': nrm(ks[1], (DEC_BATCH, DEC_SEQ, D_MODEL), 1.0),
        'norm1_g': 1.0 + nrm(ks[2], (DEPTH, D_MODEL), 0.01),
        'w_in': nrm(ks[3], (DEPTH, D_MODEL, IN_WIDTH), D_MODEL ** -0.5),
        'conv_dw_w': nrm(ks[4], (DEPTH, CONV_KERNEL, CONV_WIDTH), CONV_KERNEL ** -0.5),
        'conv_dw_b': nrm(ks[5], (DEPTH, CONV_WIDTH), 0.01),
        'conv_ln_g': 1.0 + nrm(ks[6], (DEPTH, CONV_WIDTH), 0.01),
        'conv_ln_b': nrm(ks[7], (DEPTH, CONV_WIDTH), 0.01),
        'w_out': nrm(ks[8], (DEPTH, ATTN_WIDTH + CONV_WIDTH, D_MODEL), D_MODEL ** -0.5),
        'norm2_g': 1.0 + nrm(ks[9], (DEPTH, D_MODEL), 0.01),
        'router_w_group': nrm(ks[10], (DEPTH, D_MODEL, N_EXPERT_GROUPS), D_MODEL ** -0.5),
        'router_b_group': nrm(ks[11], (DEPTH, N_EXPERT_GROUPS), 0.01),
        'router_w_expert': nrm(ks[12], (DEPTH, D_MODEL, N_EXPERTS), D_MODEL ** -0.5),
        'router_b_expert': nrm(ks[13], (DEPTH, N_EXPERTS), 0.01),
        'expert_w_gate': nrm(ks[14], (DEPTH, N_EXPERTS, D_MODEL, EXPERT_HIDDEN), D_MODEL ** -0.5),
        'expert_w_up': nrm(ks[15], (DEPTH, N_EXPERTS, D_MODEL, EXPERT_HIDDEN), D_MODEL ** -0.5),
        'expert_w_down': nrm(ks[16], (DEPTH, N_EXPERTS, EXPERT_HIDDEN, D_MODEL), EXPERT_HIDDEN ** -0.5),
        'norm_f_g': 1.0 + nrm(ks[17], (D_MODEL,), 0.01),
    }


def reference(x_prompt, x_sample, norm1_g, w_in, conv_dw_w, conv_dw_b, conv_ln_g, conv_ln_b,
              w_out, norm2_g, router_w_group, router_b_group, router_w_expert, router_b_expert,
              expert_w_gate, expert_w_up, expert_w_down, norm_f_g):
    def trunk(x):
        for l in range(DEPTH):
            x = encoder_layer(x, norm1_g[l], w_in[l], conv_dw_w[l], conv_dw_b[l],
                              conv_ln_g[l], conv_ln_b[l], w_out[l], norm2_g[l],
                              router_w_group[l], router_b_group[l], router_w_expert[l],
                              router_b_expert[l], expert_w_gate[l], expert_w_up[l],
                              expert_w_down[l])
        return rms_norm(x, norm_f_g)

    y_prompt = trunk(x_prompt)
    y_sample = trunk(x_sample)
    return (y_prompt, y_sample)
```

```python
import functools

import jax
import jax.numpy as jnp
from jax import lax
from jax.experimental import pallas as pl
from jax.experimental.pallas import tpu as pltpu

D_MODEL = 1024
HEAD_DIM = 64
HEADS_PER_GROUP = 4
GROUP_WIDTH = HEADS_PER_GROUP * HEAD_DIM
DILATIONS = (1, 4, 16)
HALF_WINDOW = 64
N_GROUPS = len(DILATIONS)
ATTN_WIDTH = N_GROUPS * GROUP_WIDTH
CONV_WIDTH = D_MODEL - ATTN_WIDTH
CONV_KERNEL = 31
CONV_HALO = 16
ROPE_THETA = 10000.0
N_EXPERT_GROUPS = 4
EXPERTS_PER_GROUP = 8
N_EXPERTS = N_EXPERT_GROUPS * EXPERTS_PER_GROUP
EXPERT_HIDDEN = D_MODEL // 2
NORM_EPS = 1e-6
NEG_INF = -1e30

LANES = 128
TOKEN_TILE = 512
ATTN_QBLOCK = 128
ATTN_KBLOCK = ATTN_QBLOCK + 2 * HALF_WINDOW
MOE_ROWS = 256
COMBINE_ROWS = 256
VMEM_LIMIT = 56 * 1024 * 1024

F32 = jnp.float32
BF16 = jnp.bfloat16


def _params(n_axes):
    return pltpu.CompilerParams(dimension_semantics=("arbitrary",) * n_axes,
                                vmem_limit_bytes=VMEM_LIMIT)


def _inproj_kernel(x_ref, g_ref, w_ref, cos_ref, sin_ref, o0_ref, o1_ref, o2_ref, glu_ref, scr_ref):
    tm = x_ref.shape[1]
    x = x_ref[0]
    u = x * lax.rsqrt(jnp.mean(x * x, axis=-1, keepdims=True) + NORM_EPS) * g_ref[...]
    ub = u.astype(BF16)
    cos = cos_ref[...]
    sin = sin_ref[...]
    lane = lax.broadcasted_iota(jnp.int32, (tm, LANES), 1)
    first_half = (lane % HEAD_DIM) < (HEAD_DIM // 2)

    def rope(t):
        partner = jnp.where(first_half, pltpu.roll(t, LANES - HEAD_DIM // 2, axis=1),
                            pltpu.roll(t, HEAD_DIM // 2, axis=1))
        return t * cos + partner * sin

    for part in range(3):
        p = jnp.dot(ub, w_ref[:, part * ATTN_WIDTH:(part + 1) * ATTN_WIDTH],
                    preferred_element_type=F32)
        for cb in range(ATTN_WIDTH // LANES):
            t = p[:, cb * LANES:(cb + 1) * LANES]
            if part == 0:
                t = rope(t) * (HEAD_DIM ** -0.5)
            elif part == 1:
                t = rope(t)
            scr_ref[part * (ATTN_WIDTH // LANES) + cb] = t
    c = jnp.dot(ub, w_ref[:, 3 * ATTN_WIDTH:], preferred_element_type=F32)
    glu_ref[0] = c[:, :CONV_WIDTH] * jax.nn.sigmoid(c[:, CONV_WIDTH:])

    lane_blocks = GROUP_WIDTH // LANES
    for part in range(3):
        for g, (d, o_ref) in enumerate(zip(DILATIONS, (o0_ref, o1_ref, o2_ref))):
            for lb in range(lane_blocks):
                cb = part * (ATTN_WIDTH // LANES) + g * lane_blocks + lb
                for c_res in range(d):
                    if d == 1:
                        rows = scr_ref[cb]
                    else:
                        rows = scr_ref[cb, pl.ds(c_res, tm // d, stride=d), :]
                    o_ref[part, 0, c_res, :, lb * LANES:(lb + 1) * LANES] = rows.astype(BF16)


def _inproj_call(x, g1, w_in_bf, cos_t, sin_t):
    B, S, D = x.shape
    tm = TOKEN_TILE
    grid = (B, S // tm)
    out_shape = [jax.ShapeDtypeStruct((3, B, d, S // d, GROUP_WIDTH), BF16) for d in DILATIONS]
    out_shape.append(jax.ShapeDtypeStruct((B, S, CONV_WIDTH), F32))
    out_specs = [pl.BlockSpec((3, 1, d, tm // d, GROUP_WIDTH), lambda b, i: (0, b, 0, i, 0))
                 for d in DILATIONS]
    out_specs.append(pl.BlockSpec((1, tm, CONV_WIDTH), lambda b, i: (b, i, 0)))
    return pl.pallas_call(
        _inproj_kernel,
        grid=grid,
        in_specs=[
            pl.BlockSpec((1, tm, D), lambda b, i: (b, i, 0)),
            pl.BlockSpec((1, D), lambda b, i: (0, 0)),
            pl.BlockSpec(w_in_bf.shape, lambda b, i: (0, 0)),
            pl.BlockSpec((tm, LANES), lambda b, i: (i, 0)),
            pl.BlockSpec((tm, LANES), lambda b, i: (i, 0)),
        ],
        out_specs=out_specs,
        out_shape=out_shape,
        scratch_shapes=[pltpu.VMEM((3 * ATTN_WIDTH // LANES, tm, LANES), F32)],
        compiler_params=_params(2),
        name="inproj",
    )(x, g1, w_in_bf, cos_t, sin_t)


def _attn_kernel(q_ref, k_ref, kl_ref, kr_ref, v_ref, vl_ref, vr_ref, o_ref, lse_ref, kx_ref, vx_ref,
                 *, seq_len):
    bq = q_ref.shape[2]
    hw = HALF_WINDOW
    q0 = pl.program_id(1) * bq
    kx_ref[0:hw] = kl_ref[0, 0]
    kx_ref[hw:hw + bq] = k_ref[0, 0]
    kx_ref[hw + bq:] = kr_ref[0, 0]
    vx_ref[0:hw] = vl_ref[0, 0]
    vx_ref[hw:hw + bq] = v_ref[0, 0]
    vx_ref[hw + bq:] = vr_ref[0, 0]

    nh = HEADS_PER_GROUP
    head_of_lane = lax.broadcasted_iota(jnp.int32, (ATTN_QBLOCK, GROUP_WIDTH), 1) // HEAD_DIM
    qq = lax.broadcasted_iota(jnp.int32, (ATTN_QBLOCK, ATTN_KBLOCK), 0)
    kk = lax.broadcasted_iota(jnp.int32, (ATTN_QBLOCK, ATTN_KBLOCK), 1)
    band = jnp.abs(kk - hw - qq) <= hw

    for j in range(bq // ATTN_QBLOCK):
        r0 = j * ATTN_QBLOCK
        qj = q_ref[0, 0, r0:r0 + ATTN_QBLOCK, :]
        kj = kx_ref[r0:r0 + ATTN_KBLOCK, :]
        vj = vx_ref[r0:r0 + ATTN_KBLOCK, :]
        kpos = kk + (q0 + r0 - hw)
        valid = band & (kpos >= 0) & (kpos < seq_len)
        qs = jnp.concatenate([jnp.where(head_of_lane == h, qj, jnp.zeros_like(qj)) for h in range(nh)],
                             axis=0)
        s = lax.dot_general(qs, kj, (((1,), (1,)), ((), ())), preferred_element_type=F32)
        s = jnp.where(jnp.concatenate([valid] * nh, axis=0), s, NEG_INF)
        m = jnp.max(s, axis=-1, keepdims=True)
        p = jnp.exp(s - m)
        den = jnp.sum(p, axis=-1, keepdims=True)
        r = jnp.dot(p.astype(BF16), vj, preferred_element_type=F32) / den
        lse = m + jnp.log(den)
        out = jnp.zeros((ATTN_QBLOCK, GROUP_WIDTH), F32)
        lse_full = jnp.zeros((ATTN_QBLOCK, GROUP_WIDTH), F32)
        for h in range(nh):
            sel = head_of_lane == h
            out = jnp.where(sel, r[h * ATTN_QBLOCK:(h + 1) * ATTN_QBLOCK], out)
            lse_full = jnp.where(sel, lse[h * ATTN_QBLOCK:(h + 1) * ATTN_QBLOCK], lse_full)
        o_ref[0, r0:r0 + ATTN_QBLOCK, :] = out.astype(BF16)
        lse_ref[0, r0:r0 + ATTN_QBLOCK, :] = lse_full


def _attn_call(qkv):
    _, nseq, L, W = qkv.shape
    bq = min(L, TOKEN_TILE)
    hw = HALF_WINDOW
    nhb = bq // hw
    last_hb = L // hw - 1

    def main(part):
        return pl.BlockSpec((1, 1, bq, W), lambda n, i: (part, n, i, 0))

    def left(part):
        return pl.BlockSpec((1, 1, hw, W), lambda n, i: (part, n, jnp.maximum(i * nhb - 1, 0), 0))

    def right(part):
        return pl.BlockSpec((1, 1, hw, W), lambda n, i: (part, n, jnp.minimum((i + 1) * nhb, last_hb), 0))

    return pl.pallas_call(
        functools.partial(_attn_kernel, seq_len=L),
        grid=(nseq, L // bq),
        in_specs=[main(0), main(1), left(1), right(1), main(2), left(2), right(2)],
        out_specs=[pl.BlockSpec((1, bq, W), lambda n, i: (n, i, 0)),
                   pl.BlockSpec((1, bq, W), lambda n, i: (n, i, 0))],
        out_shape=[jax.ShapeDtypeStruct((nseq, L, W), BF16),
                   jax.ShapeDtypeStruct((nseq, L, W), F32)],
        scratch_shapes=[pltpu.VMEM((bq + 2 * hw, W), BF16), pltpu.VMEM((bq + 2 * hw, W), BF16)],
        compiler_params=_params(2),
        name="band_attn",
    )(qkv, qkv, qkv, qkv, qkv, qkv, qkv)


def _mix_kernel(x_ref, o0_ref, l0_ref, o1_ref, l1_ref, o2_ref, l2_ref, glu_ref, glul_ref, glur_ref,
                cw_ref, cb_ref, lng_ref, lnb_ref, wo_ref, g2_ref, wr_ref, br_ref, tri_ref,
                h_ref, xn_ref, route_ref, cnt_ref,
                cat_ref, nat_ref, hpad_ref, carry_ref):
    tm = x_ref.shape[1]
    b = pl.program_id(0)
    i = pl.program_id(1)
    n_i = pl.num_programs(1)

    @pl.when((b == 0) & (i == 0))
    def _():
        carry_ref[...] = jnp.zeros_like(carry_ref)

    for slot, (src, d) in enumerate(((o1_ref, DILATIONS[1]), (l1_ref, DILATIONS[1]),
                                     (o2_ref, DILATIONS[2]), (l2_ref, DILATIONS[2]))):
        for c_res in range(d):
            for lb in range(GROUP_WIDTH // LANES):
                nat_ref[slot, lb, pl.ds(c_res, tm // d, stride=d), :] = (
                    src[0, c_res, :, lb * LANES:(lb + 1) * LANES].astype(F32))

    def natural(slot):
        return jnp.concatenate([nat_ref[slot, lb] for lb in range(GROUP_WIDTH // LANES)], axis=1)

    outs = (o0_ref[0, 0].astype(F32), natural(0), natural(2))
    lses = (l0_ref[0, 0], natural(1), natural(3))
    lmax = jnp.maximum(jnp.maximum(lses[0], lses[1]), lses[2])
    es = [jnp.exp(l - lmax) for l in lses]
    inv = 1.0 / (es[0] + es[1] + es[2])
    for g in range(N_GROUPS):
        cat_ref[:, g * GROUP_WIDTH:(g + 1) * GROUP_WIDTH] = (outs[g] * (es[g] * inv)).astype(BF16)

    zeros_halo = jnp.zeros((CONV_HALO, CONV_WIDTH), F32)
    hpad_ref[0:CONV_HALO] = jnp.where(i > 0, glul_ref[0], zeros_halo)
    hpad_ref[CONV_HALO:CONV_HALO + tm] = glu_ref[0]
    hpad_ref[CONV_HALO + tm:] = jnp.where(i < n_i - 1, glur_ref[0], zeros_halo)
    chunk = 64
    off0 = CONV_HALO - CONV_KERNEL // 2
    for r0 in range(0, tm, chunk):
        acc = jnp.zeros((chunk, CONV_WIDTH), F32) + cb_ref[...]
        for j in range(CONV_KERNEL):
            acc = acc + hpad_ref[r0 + off0 + j: r0 + off0 + j + chunk, :] * cw_ref[j:j + 1, :]
        mu = jnp.mean(acc, axis=-1, keepdims=True)
        xc = acc - mu
        var = jnp.mean(xc * xc, axis=-1, keepdims=True)
        y = xc * lax.rsqrt(var + NORM_EPS) * lng_ref[...] + lnb_ref[...]
        cat_ref[r0:r0 + chunk, ATTN_WIDTH:] = (y * jax.nn.sigmoid(y)).astype(BF16)

    h = x_ref[0] + jnp.dot(cat_ref[...], wo_ref[...], preferred_element_type=F32)
    h_ref[0] = h
    xn = h * lax.rsqrt(jnp.mean(h * h, axis=-1, keepdims=True) + NORM_EPS) * g2_ref[...]
    xn_ref[0] = xn

    xh = xn.astype(BF16)
    xl = (xn - xh.astype(F32)).astype(BF16)
    rr = jnp.dot(jnp.concatenate([xh, xl], axis=0), wr_ref[...], preferred_element_type=F32)
    lg = (rr[:tm, :LANES] + rr[:tm, LANES:]) + (rr[tm:, :LANES] + rr[tm:, LANES:]) + br_ref[...]

    lane = lax.broadcasted_iota(jnp.int32, (tm, LANES), 1)
    big = jnp.int32(1 << 20)
    is_g = (lane >= N_EXPERTS) & (lane < N_EXPERTS + N_EXPERT_GROUPS)
    gl = jnp.where(is_g, lg, -jnp.inf)
    gmax = jnp.max(gl, axis=-1, keepdims=True)
    grp = jnp.min(jnp.where(gl == gmax, lane, big), axis=-1, keepdims=True) - N_EXPERTS
    p_grp = 1.0 / jnp.sum(jnp.where(is_g, jnp.exp(lg - gmax), 0.0), axis=-1, keepdims=True)
    in_grp = (lane >= grp * EXPERTS_PER_GROUP) & (lane < (grp + 1) * EXPERTS_PER_GROUP)
    el = jnp.where(in_grp, lg, -jnp.inf)
    v1 = jnp.max(el, axis=-1, keepdims=True)
    i1 = jnp.min(jnp.where(el == v1, lane, big), axis=-1, keepdims=True)
    el2 = jnp.where(lane == i1, -jnp.inf, el)
    v2 = jnp.max(el2, axis=-1, keepdims=True)
    i2 = jnp.min(jnp.where(el2 == v2, lane, big), axis=-1, keepdims=True)
    t = jnp.exp(v2 - v1)
    w1 = p_grp / (1.0 + t)
    w2 = p_grp * t / (1.0 + t)

    hit1 = lane == i1
    hit2 = lane == i2
    onehot = jnp.where(hit1 | hit2, 1.0, 0.0)
    tot = carry_ref[...] + jnp.dot(tri_ref[...], onehot.astype(BF16), preferred_element_type=F32)
    r1 = jnp.sum(jnp.where(hit1, tot, 0.0), axis=-1, keepdims=True)
    r2 = jnp.sum(jnp.where(hit2, tot, 0.0), axis=-1, keepdims=True)
    carry_ref[...] = carry_ref[...] + jnp.sum(onehot, axis=0, keepdims=True)
    cnt_ref[...] = carry_ref[...]

    route = jnp.where(lane == 0, i1.astype(F32), 0.0)
    route = jnp.where(lane == 1, i2.astype(F32), route)
    route = jnp.where(lane == 2, r1, route)
    route = jnp.where(lane == 3, r2, route)
    route = jnp.where(lane == 4, w1, route)
    route = jnp.where(lane == 5, w2, route)
    route_ref[0] = route


def _mix_call(x, attn_out, glu, conv_w, conv_b, ln_g, ln_b, w_out_bf, g2, wr, br, tri):
    B, S, D = x.shape
    tm = TOKEN_TILE
    nhalo = tm // CONV_HALO
    last_halo = S // CONV_HALO - 1
    in_specs = [pl.BlockSpec((1, tm, D), lambda b, i: (b, i, 0))]
    args = [x]
    for d, (o, l) in zip(DILATIONS, attn_out):
        spec = pl.BlockSpec((1, d, tm // d, GROUP_WIDTH), lambda b, i: (b, 0, i, 0))
        in_specs += [spec, spec]
        args += [o.reshape(B, d, S // d, GROUP_WIDTH), l.reshape(B, d, S // d, GROUP_WIDTH)]
    in_specs += [
        pl.BlockSpec((1, tm, CONV_WIDTH), lambda b, i: (b, i, 0)),
        pl.BlockSpec((1, CONV_HALO, CONV_WIDTH), lambda b, i: (b, jnp.maximum(i * nhalo - 1, 0), 0)),
        pl.BlockSpec((1, CONV_HALO, CONV_WIDTH),
                     lambda b, i: (b, jnp.minimum((i + 1) * nhalo, last_halo), 0)),
    ]
    args += [glu, glu, glu]
    for a in (conv_w, conv_b, ln_g, ln_b, w_out_bf, g2, wr, br, tri):
        in_specs.append(pl.BlockSpec(a.shape, lambda b, i: (0, 0)))
        args.append(a)
    return pl.pallas_call(
        _mix_kernel,
        grid=(B, S // tm),
        in_specs=in_specs,
        out_specs=[pl.BlockSpec((1, tm, D), lambda b, i: (b, i, 0)),
                   pl.BlockSpec((1, tm, D), lambda b, i: (b, i, 0)),
                   pl.BlockSpec((1, tm, LANES), lambda b, i: (b, i, 0)),
                   pl.BlockSpec((1, LANES), lambda b, i: (0, 0))],
        out_shape=[jax.ShapeDtypeStruct((B, S, D), F32),
                   jax.ShapeDtypeStruct((B, S, D), F32),
                   jax.ShapeDtypeStruct((B, S, LANES), F32),
                   jax.ShapeDtypeStruct((1, LANES), F32)],
        scratch_shapes=[pltpu.VMEM((tm, D), BF16),
                        pltpu.VMEM((4, GROUP_WIDTH // LANES, tm, LANES), F32),
                        pltpu.VMEM((tm + 2 * CONV_HALO, CONV_WIDTH), F32),
                        pltpu.VMEM((1, LANES), F32)],
        compiler_params=_params(2),
        name="mix_outproj_router",
    )(*args)


def _row_copy(src_hbm, row, dst_vmem, slot, sem):
    return pltpu.make_async_copy(src_hbm.at[pl.ds(row, 1), :], dst_vmem.at[pl.ds(slot, 1), :], sem)


def _expert_kernel(be_ref, rt_ref, xn_hbm, wg_ref, wu_ref, wd_ref, y_ref, xbuf, sem):
    del be_ref
    base = pl.program_id(0) * MOE_ROWS

    def issue(r, carry):
        _row_copy(xn_hbm, rt_ref[base + r], xbuf, r, sem).start()
        return carry

    lax.fori_loop(0, MOE_ROWS, issue, 0)

    def drain(r, carry):
        _row_copy(xn_hbm, 0, xbuf, r, sem).wait()
        return carry

    lax.fori_loop(0, MOE_ROWS, drain, 0)

    xb = xbuf[...].astype(BF16)
    hg = jnp.dot(xb, wg_ref[0], preferred_element_type=F32)
    hu = jnp.dot(xb, wu_ref[0], preferred_element_type=F32)
    hh = (hg * jax.nn.sigmoid(hg) * hu).astype(BF16)
    y_ref[...] = jnp.dot(hh, wd_ref[0], preferred_element_type=F32)


def _expert_call(block_expert, row_tok, xn, wg_bf, wu_bf, wd_bf):
    n_blocks = block_expert.shape[0]
    T, D = xn.shape
    grid_spec = pltpu.PrefetchScalarGridSpec(
        num_scalar_prefetch=2,
        grid=(n_blocks,),
        in_specs=[
            pl.BlockSpec(memory_space=pl.ANY),
            pl.BlockSpec((1, D, EXPERT_HIDDEN), lambda i, be, rt: (be[i], 0, 0)),
            pl.BlockSpec((1, D, EXPERT_HIDDEN), lambda i, be, rt: (be[i], 0, 0)),
            pl.BlockSpec((1, EXPERT_HIDDEN, D), lambda i, be, rt: (be[i], 0, 0)),
        ],
        out_specs=pl.BlockSpec((MOE_ROWS, D), lambda i, be, rt: (i, 0)),
        scratch_shapes=[pltpu.VMEM((MOE_ROWS, D), F32), pltpu.SemaphoreType.DMA],
    )
    return pl.pallas_call(
        _expert_kernel,
        grid_spec=grid_spec,
        out_shape=jax.ShapeDtypeStruct((n_blocks * MOE_ROWS, D), F32),
        compiler_params=_params(1),
        name="moe_experts",
    )(block_expert, row_tok, xn, wg_bf, wu_bf, wd_bf)


def _combine_kernel(dest_ref, h_ref, route_ref, y_hbm, gf_ref, o_ref, ybuf, sem):
    rows = h_ref.shape[0]
    base = pl.program_id(0) * rows

    def issue(r, carry):
        for k in range(2):
            _row_copy(y_hbm, dest_ref[2 * (base + r) + k], ybuf.at[k], r, sem).start()
        return carry

    lax.fori_loop(0, rows, issue, 0)

    def drain(r, carry):
        for k in range(2):
            _row_copy(y_hbm, 0, ybuf.at[k], r, sem).wait()
        return carry

    lax.fori_loop(0, rows, drain, 0)

    route = route_ref[...]
    w1 = route[:, 4:5]
    w2 = route[:, 5:6]
    z = h_ref[...] + (w1 * ybuf[0] + w2 * ybuf[1])
    o_ref[...] = z * lax.rsqrt(jnp.mean(z * z, axis=-1, keepdims=True) + NORM_EPS) * gf_ref[...]


def _combine_call(dest, h, route, y, gf):
    T, D = h.shape
    rows = COMBINE_ROWS
    grid_spec = pltpu.PrefetchScalarGridSpec(
        num_scalar_prefetch=1,
        grid=(T // rows,),
        in_specs=[
            pl.BlockSpec((rows, D), lambda i, d: (i, 0)),
            pl.BlockSpec((rows, LANES), lambda i, d: (i, 0)),
            pl.BlockSpec(memory_space=pl.ANY),
            pl.BlockSpec((1, D), lambda i, d: (0, 0)),
        ],
        out_specs=pl.BlockSpec((rows, D), lambda i, d: (i, 0)),
        scratch_shapes=[pltpu.VMEM((2, rows, D), F32), pltpu.SemaphoreType.DMA],
    )
    return pl.pallas_call(
        _combine_kernel,
        grid_spec=grid_spec,
        out_shape=jax.ShapeDtypeStruct((T, D), F32),
        compiler_params=_params(1),
        name="moe_combine_norm",
    )(dest, h, route, y, gf)


def _rope_tables(S):
    half = HEAD_DIM // 2
    inv_freq = ROPE_THETA ** (-jnp.arange(half, dtype=F32) / half)
    ang = jnp.arange(S, dtype=F32)[:, None] * inv_freq[None, :]
    cos = jnp.cos(ang)
    sin = jnp.sin(ang)
    reps = LANES // HEAD_DIM
    return jnp.tile(cos, (1, 2 * reps)), jnp.tile(jnp.concatenate([-sin, sin], axis=1), (1, reps))


def _router_weights(w_rg, b_rg, w_re, b_re):
    D = w_re.shape[0]
    w = jnp.zeros((D, LANES), F32).at[:, :N_EXPERTS].set(w_re)
    w = w.at[:, N_EXPERTS:N_EXPERTS + N_EXPERT_GROUPS].set(w_rg)
    hi = w.astype(BF16)
    lo = (w - hi.astype(F32)).astype(BF16)
    b = jnp.zeros((1, LANES), F32).at[0, :N_EXPERTS].set(b_re)
    b = b.at[0, N_EXPERTS:N_EXPERTS + N_EXPERT_GROUPS].set(b_rg)
    return jnp.concatenate([hi, lo], axis=1), b


def _trunk(x, p):
    B, S, D = x.shape
    T = B * S
    cos_t, sin_t = _rope_tables(S)
    o0, o1, o2, glu = _inproj_call(x, p["g1"], p["w_in"], cos_t, sin_t)
    attn_out = []
    for d, qkv in zip(DILATIONS, (o0, o1, o2)):
        attn_out.append(_attn_call(qkv.reshape(3, B * d, S // d, GROUP_WIDTH)))
    h, xn, route, counts = _mix_call(x, attn_out, glu, p["conv_w"], p["conv_b"], p["ln_g"], p["ln_b"],
                                     p["w_out"], p["g2"], p["wr"], p["br"], p["tri"])
    h = h.reshape(T, D)
    xn = xn.reshape(T, D)
    route = route.reshape(T, LANES)

    eid = route[:, 0:2].astype(jnp.int32)
    rank = route[:, 2:4].astype(jnp.int32)
    cnt = counts[0, :N_EXPERTS].astype(jnp.int32)
    pcnt = (cnt + MOE_ROWS - 1) // MOE_ROWS * MOE_ROWS
    pend = jnp.cumsum(pcnt)
    pstart = pend - pcnt
    dest = (pstart[eid] + rank).reshape(2 * T)
    n_blocks = (2 * T) // MOE_ROWS + N_EXPERTS
    tok = jnp.repeat(jnp.arange(T, dtype=jnp.int32), 2)
    row_tok = jnp.zeros((n_blocks * MOE_ROWS,), jnp.int32).at[dest].set(tok)
    block_expert = jnp.minimum(
        jnp.searchsorted(pend, jnp.arange(n_blocks, dtype=jnp.int32) * MOE_ROWS, side="right"),
        N_EXPERTS - 1).astype(jnp.int32)

    y = _expert_call(block_expert, row_tok, xn, p["wg"], p["wu"], p["wd"])
    out = _combine_call(dest, h, route, y, p["gf"])
    return out.reshape(B, S, D)


def kernel(x_prompt, x_sample, norm1_g, w_in, conv_dw_w, conv_dw_b, conv_ln_g, conv_ln_b, w_out, norm2_g,
           router_w_group, router_b_group, router_w_expert, router_b_expert, expert_w_gate, expert_w_up,
           expert_w_down, norm_f_g):
    assert norm1_g.shape[0] == 1, "one encoder layer"
    wr, br = _router_weights(router_w_group[0], router_b_group[0], router_w_expert[0], router_b_expert[0])
    tm = TOKEN_TILE
    tri = (lax.broadcasted_iota(jnp.int32, (tm, tm), 1)
           < lax.broadcasted_iota(jnp.int32, (tm, tm), 0)).astype(BF16)
    p = {
        "g1": norm1_g[0][None, :],
        "w_in": w_in[0].astype(BF16),
        "conv_w": jnp.pad(conv_dw_w[0], ((0, 1), (0, 0))),
        "conv_b": conv_dw_b[0][None, :],
        "ln_g": conv_ln_g[0][None, :],
        "ln_b": conv_ln_b[0][None, :],
        "w_out": w_out[0].astype(BF16),
        "g2": norm2_g[0][None, :],
        "wr": wr,
        "br": br,
        "tri": tri,
        "wg": expert_w_gate[0].astype(BF16),
        "wu": expert_w_up[0].astype(BF16),
        "wd": expert_w_down[0].astype(BF16),
        "gf": norm_f_g[None, :],
    }
    return (_trunk(x_prompt, p), _trunk(x_sample, p))
```

```python
import functools

import jax
import jax.numpy as jnp
from jax import lax
from jax.experimental import pallas as pl
from jax.experimental.pallas import tpu as pltpu

D_MODEL = 1024
HEAD_DIM = 64
HEADS_PER_GROUP = 4
GROUP_WIDTH = HEADS_PER_GROUP * HEAD_DIM
DILATIONS = (1, 4, 16)
HALF_WINDOW = 64
N_GROUPS = len(DILATIONS)
ATTN_WIDTH = N_GROUPS * GROUP_WIDTH
CONV_WIDTH = D_MODEL - ATTN_WIDTH
CONV_KERNEL = 31
CONV_HALO = 16
ROPE_THETA = 10000.0
N_EXPERT_GROUPS = 4
EXPERTS_PER_GROUP = 8
N_EXPERTS = N_EXPERT_GROUPS * EXPERTS_PER_GROUP
EXPERT_HIDDEN = D_MODEL // 2
NORM_EPS = 1e-6
NEG_INF = -1e30

LANES = 128
TOKEN_TILE = 512
ATTN_QBLOCK = 128
ATTN_KBLOCK = ATTN_QBLOCK + 2 * HALF_WINDOW
MOE_ROWS = 256
DISPATCH_ROWS = 512
COMBINE_ROWS = 256
DMA_UNROLL = 8
VMEM_LIMIT = 56 * 1024 * 1024

F32 = jnp.float32
BF16 = jnp.bfloat16


def _params(n_axes):
    return pltpu.CompilerParams(dimension_semantics=("arbitrary",) * n_axes,
                                vmem_limit_bytes=VMEM_LIMIT)


def _inproj_kernel(x_ref, g_ref, w_ref, cos_ref, sin_ref, o0_ref, o1_ref, o2_ref, glu_ref, scr_ref):
    tm = x_ref.shape[1]
    x = x_ref[0]
    u = x * lax.rsqrt(jnp.mean(x * x, axis=-1, keepdims=True) + NORM_EPS) * g_ref[...]
    ub = u.astype(BF16)
    cos = cos_ref[...]
    sin = sin_ref[...]
    lane = lax.broadcasted_iota(jnp.int32, (tm, LANES), 1)
    first_half = (lane % HEAD_DIM) < (HEAD_DIM // 2)

    def rope(t):
        partner = jnp.where(first_half, pltpu.roll(t, LANES - HEAD_DIM // 2, axis=1),
                            pltpu.roll(t, HEAD_DIM // 2, axis=1))
        return t * cos + partner * sin

    for part in range(3):
        p = jnp.dot(ub, w_ref[:, part * ATTN_WIDTH:(part + 1) * ATTN_WIDTH],
                    preferred_element_type=F32)
        for cb in range(ATTN_WIDTH // LANES):
            t = p[:, cb * LANES:(cb + 1) * LANES]
            if part == 0:
                t = rope(t) * (HEAD_DIM ** -0.5)
            elif part == 1:
                t = rope(t)
            scr_ref[part * (ATTN_WIDTH // LANES) + cb] = t
    c = jnp.dot(ub, w_ref[:, 3 * ATTN_WIDTH:], preferred_element_type=F32)
    glu_ref[0] = c[:, :CONV_WIDTH] * jax.nn.sigmoid(c[:, CONV_WIDTH:])

    lane_blocks = GROUP_WIDTH // LANES
    for part in range(3):
        for g, (d, o_ref) in enumerate(zip(DILATIONS, (o0_ref, o1_ref, o2_ref))):
            for lb in range(lane_blocks):
                cb = part * (ATTN_WIDTH // LANES) + g * lane_blocks + lb
                for c_res in range(d):
                    if d == 1:
                        rows = scr_ref[cb]
                    else:
                        rows = scr_ref[cb, pl.ds(c_res, tm // d, stride=d), :]
                    o_ref[part, 0, c_res, :, lb * LANES:(lb + 1) * LANES] = rows.astype(BF16)


def _inproj_call(x, g1, w_in_bf, cos_t, sin_t):
    B, S, D = x.shape
    tm = TOKEN_TILE
    grid = (B, S // tm)
    out_shape = [jax.ShapeDtypeStruct((3, B, d, S // d, GROUP_WIDTH), BF16) for d in DILATIONS]
    out_shape.append(jax.ShapeDtypeStruct((B, S, CONV_WIDTH), F32))
    out_specs = [pl.BlockSpec((3, 1, d, tm // d, GROUP_WIDTH), lambda b, i: (0, b, 0, i, 0))
                 for d in DILATIONS]
    out_specs.append(pl.BlockSpec((1, tm, CONV_WIDTH), lambda b, i: (b, i, 0)))
    return pl.pallas_call(
        _inproj_kernel,
        grid=grid,
        in_specs=[
            pl.BlockSpec((1, tm, D), lambda b, i: (b, i, 0)),
            pl.BlockSpec((1, D), lambda b, i: (0, 0)),
            pl.BlockSpec(w_in_bf.shape, lambda b, i: (0, 0)),
            pl.BlockSpec((tm, LANES), lambda b, i: (i, 0)),
            pl.BlockSpec((tm, LANES), lambda b, i: (i, 0)),
        ],
        out_specs=out_specs,
        out_shape=out_shape,
        scratch_shapes=[pltpu.VMEM((3 * ATTN_WIDTH // LANES, tm, LANES), F32)],
        compiler_params=_params(2),
        name="inproj",
    )(x, g1, w_in_bf, cos_t, sin_t)


def _attn_kernel(q_ref, k_ref, kl_ref, kr_ref, v_ref, vl_ref, vr_ref, o_ref, lse_ref, kx_ref, vx_ref,
                 *, seq_len):
    bq = q_ref.shape[2]
    hw = HALF_WINDOW
    q0 = pl.program_id(1) * bq
    kx_ref[0:hw] = kl_ref[0, 0]
    kx_ref[hw:hw + bq] = k_ref[0, 0]
    kx_ref[hw + bq:] = kr_ref[0, 0]
    vx_ref[0:hw] = vl_ref[0, 0]
    vx_ref[hw:hw + bq] = v_ref[0, 0]
    vx_ref[hw + bq:] = vr_ref[0, 0]

    nh = HEADS_PER_GROUP
    head_of_lane = lax.broadcasted_iota(jnp.int32, (ATTN_QBLOCK, GROUP_WIDTH), 1) // HEAD_DIM
    qq = lax.broadcasted_iota(jnp.int32, (ATTN_QBLOCK, ATTN_KBLOCK), 0)
    kk = lax.broadcasted_iota(jnp.int32, (ATTN_QBLOCK, ATTN_KBLOCK), 1)
    band = jnp.abs(kk - hw - qq) <= hw

    for j in range(bq // ATTN_QBLOCK):
        r0 = j * ATTN_QBLOCK
        qj = q_ref[0, 0, r0:r0 + ATTN_QBLOCK, :]
        kj = kx_ref[r0:r0 + ATTN_KBLOCK, :]
        vj = vx_ref[r0:r0 + ATTN_KBLOCK, :]
        kpos = kk + (q0 + r0 - hw)
        valid = band & (kpos >= 0) & (kpos < seq_len)
        qs = jnp.concatenate([jnp.where(head_of_lane == h, qj, jnp.zeros_like(qj)) for h in range(nh)],
                             axis=0)
        s = lax.dot_general(qs, kj, (((1,), (1,)), ((), ())), preferred_element_type=F32)
        s = jnp.where(jnp.concatenate([valid] * nh, axis=0), s, NEG_INF)
        m = jnp.max(s, axis=-1, keepdims=True)
        p = jnp.exp(s - m)
        den = jnp.sum(p, axis=-1, keepdims=True)
        r = jnp.dot(p.astype(BF16), vj, preferred_element_type=F32) / den
        lse = m + jnp.log(den)
        out = jnp.zeros((ATTN_QBLOCK, GROUP_WIDTH), F32)
        lse_full = jnp.zeros((ATTN_QBLOCK, GROUP_WIDTH), F32)
        for h in range(nh):
            sel = head_of_lane == h
            out = jnp.where(sel, r[h * ATTN_QBLOCK:(h + 1) * ATTN_QBLOCK], out)
            lse_full = jnp.where(sel, lse[h * ATTN_QBLOCK:(h + 1) * ATTN_QBLOCK], lse_full)
        o_ref[0, r0:r0 + ATTN_QBLOCK, :] = out.astype(BF16)
        lse_ref[0, r0:r0 + ATTN_QBLOCK, :] = lse_full


def _attn_call(qkv):
    _, nseq, L, W = qkv.shape
    bq = min(L, TOKEN_TILE)
    hw = HALF_WINDOW
    nhb = bq // hw
    last_hb = L // hw - 1

    def main(part):
        return pl.BlockSpec((1, 1, bq, W), lambda n, i: (part, n, i, 0))

    def left(part):
        return pl.BlockSpec((1, 1, hw, W), lambda n, i: (part, n, jnp.maximum(i * nhb - 1, 0), 0))

    def right(part):
        return pl.BlockSpec((1, 1, hw, W), lambda n, i: (part, n, jnp.minimum((i + 1) * nhb, last_hb), 0))

    return pl.pallas_call(
        functools.partial(_attn_kernel, seq_len=L),
        grid=(nseq, L // bq),
        in_specs=[main(0), main(1), left(1), right(1), main(2), left(2), right(2)],
        out_specs=[pl.BlockSpec((1, bq, W), lambda n, i: (n, i, 0)),
                   pl.BlockSpec((1, bq, W), lambda n, i: (n, i, 0))],
        out_shape=[jax.ShapeDtypeStruct((nseq, L, W), BF16),
                   jax.ShapeDtypeStruct((nseq, L, W), F32)],
        scratch_shapes=[pltpu.VMEM((bq + 2 * hw, W), BF16), pltpu.VMEM((bq + 2 * hw, W), BF16)],
        compiler_params=_params(2),
        name="band_attn",
    )(qkv, qkv, qkv, qkv, qkv, qkv, qkv)


def _mix_kernel(x_ref, o0_ref, l0_ref, o1_ref, l1_ref, o2_ref, l2_ref, glu_ref, glul_ref, glur_ref,
                cw_ref, cb_ref, lng_ref, lnb_ref, wo_ref, g2_ref, wr_ref, br_ref, tri_ref,
                h_ref, xn_ref, route_ref, cnt_ref,
                cat_ref, nat_ref, hpad_ref, carry_ref):
    tm = x_ref.shape[1]
    b = pl.program_id(0)
    i = pl.program_id(1)
    n_i = pl.num_programs(1)

    @pl.when((b == 0) & (i == 0))
    def _():
        carry_ref[...] = jnp.zeros_like(carry_ref)

    for slot, (src, d) in enumerate(((o1_ref, DILATIONS[1]), (l1_ref, DILATIONS[1]),
                                     (o2_ref, DILATIONS[2]), (l2_ref, DILATIONS[2]))):
        for c_res in range(d):
            for lb in range(GROUP_WIDTH // LANES):
                nat_ref[slot, lb, pl.ds(c_res, tm // d, stride=d), :] = (
                    src[0, c_res, :, lb * LANES:(lb + 1) * LANES].astype(F32))

    def natural(slot):
        return jnp.concatenate([nat_ref[slot, lb] for lb in range(GROUP_WIDTH // LANES)], axis=1)

    outs = (o0_ref[0, 0].astype(F32), natural(0), natural(2))
    lses = (l0_ref[0, 0], natural(1), natural(3))
    lmax = jnp.maximum(jnp.maximum(lses[0], lses[1]), lses[2])
    es = [jnp.exp(l - lmax) for l in lses]
    inv = 1.0 / (es[0] + es[1] + es[2])
    for g in range(N_GROUPS):
        cat_ref[:, g * GROUP_WIDTH:(g + 1) * GROUP_WIDTH] = (outs[g] * (es[g] * inv)).astype(BF16)

    zeros_halo = jnp.zeros((CONV_HALO, CONV_WIDTH), F32)
    left = jnp.where(i > 0, glul_ref[0], zeros_halo)
    right = jnp.where(i < n_i - 1, glur_ref[0], zeros_halo)
    conv_blocks = CONV_WIDTH // LANES
    for cb in range(conv_blocks):
        cols = slice(cb * LANES, (cb + 1) * LANES)
        hpad_ref[cb, 0:CONV_HALO] = left[:, cols]
        hpad_ref[cb, CONV_HALO:CONV_HALO + tm] = glu_ref[0, :, cols]
        hpad_ref[cb, CONV_HALO + tm:] = right[:, cols]
    chunk = 64
    off0 = CONV_HALO - CONV_KERNEL // 2
    for r0 in range(0, tm, chunk):
        accs = []
        for cb in range(conv_blocks):
            cols = slice(cb * LANES, (cb + 1) * LANES)
            a = jnp.zeros((chunk, LANES), F32) + cb_ref[:, cols]
            for j in range(CONV_KERNEL):
                a = a + hpad_ref[cb, r0 + off0 + j: r0 + off0 + j + chunk, :] * cw_ref[j:j + 1, cols]
            accs.append(a)
        acc = jnp.concatenate(accs, axis=1)
        mu = jnp.mean(acc, axis=-1, keepdims=True)
        xc = acc - mu
        var = jnp.mean(xc * xc, axis=-1, keepdims=True)
        y = xc * lax.rsqrt(var + NORM_EPS) * lng_ref[...] + lnb_ref[...]
        cat_ref[r0:r0 + chunk, ATTN_WIDTH:] = (y * jax.nn.sigmoid(y)).astype(BF16)

    h = x_ref[0] + jnp.dot(cat_ref[...], wo_ref[...], preferred_element_type=F32)
    h_ref[0] = h
    xn = h * lax.rsqrt(jnp.mean(h * h, axis=-1, keepdims=True) + NORM_EPS) * g2_ref[...]

    xn_ref[0] = xn
    xh = xn.astype(BF16)
    xl = (xn - xh.astype(F32)).astype(BF16)
    rr =jnp.dot(jnp.concatenate([xh, xl], axis=0), wr_ref[...], preferred_element_type=F32)
    lg = (rr[:tm, :LANES] + rr[:tm, LANES:]) + (rr[tm:, :LANES] + rr[tm:, LANES:]) + br_ref[...]

    lane = lax.broadcasted_iota(jnp.int32, (tm, LANES), 1)
    big = jnp.int32(1 << 20)
    is_g = (lane >= N_EXPERTS) & (lane < N_EXPERTS + N_EXPERT_GROUPS)
    gl = jnp.where(is_g, lg, -jnp.inf)
    gmax = jnp.max(gl, axis=-1, keepdims=True)
    grp = jnp.min(jnp.where(gl == gmax, lane, big), axis=-1, keepdims=True) - N_EXPERTS
    p_grp = 1.0 / jnp.sum(jnp.where(is_g, jnp.exp(lg - gmax), 0.0), axis=-1, keepdims=True)
    in_grp = (lane >= grp * EXPERTS_PER_GROUP) & (lane < (grp + 1) * EXPERTS_PER_GROUP)
    el = jnp.where(in_grp, lg, -jnp.inf)
    v1 = jnp.max(el, axis=-1, keepdims=True)
    i1 = jnp.min(jnp.where(el == v1, lane, big), axis=-1, keepdims=True)
    el2 = jnp.where(lane == i1, -jnp.inf, el)
    v2 = jnp.max(el2, axis=-1, keepdims=True)
    i2 = jnp.min(jnp.where(el2 == v2, lane, big), axis=-1, keepdims=True)
    t = jnp.exp(v2 - v1)
    w1 = p_grp / (1.0 + t)
    w2 = p_grp * t / (1.0 + t)

    hit1 = lane == i1
    hit2 = lane == i2
    onehot = jnp.where(hit1 | hit2, 1.0, 0.0)
    tot = carry_ref[...] + jnp.dot(tri_ref[...], onehot.astype(BF16), preferred_element_type=F32)
    r1 = jnp.sum(jnp.where(hit1, tot, 0.0), axis=-1, keepdims=True)
    r2 = jnp.sum(jnp.where(hit2, tot, 0.0), axis=-1, keepdims=True)
    carry_ref[...] = carry_ref[...] + jnp.sum(onehot, axis=0, keepdims=True)
    cnt_ref[...] = carry_ref[...]

    route = jnp.where(lane == 0, i1.astype(F32), 0.0)
    route = jnp.where(lane == 1, i2.astype(F32), route)
    route = jnp.where(lane == 2, r1, route)
    route = jnp.where(lane == 3, r2, route)
    route = jnp.where(lane == 4, w1, route)
    route = jnp.where(lane == 5, w2, route)
    route_ref[0] = route


def _mix_call(x, attn_out, glu, conv_w, conv_b, ln_g, ln_b, w_out_bf, g2, wr, br, tri):
    B, S, D = x.shape
    tm = TOKEN_TILE
    nhalo = tm // CONV_HALO
    last_halo = S // CONV_HALO - 1
    in_specs = [pl.BlockSpec((1, tm, D), lambda b, i: (b, i, 0))]
    args = [x]
    for d, (o, l) in zip(DILATIONS, attn_out):
        spec = pl.BlockSpec((1, d, tm // d, GROUP_WIDTH), lambda b, i: (b, 0, i, 0))
        in_specs += [spec, spec]
        args += [o.reshape(B, d, S // d, GROUP_WIDTH), l.reshape(B, d, S // d, GROUP_WIDTH)]
    in_specs += [
        pl.BlockSpec((1, tm, CONV_WIDTH), lambda b, i: (b, i, 0)),
        pl.BlockSpec((1, CONV_HALO, CONV_WIDTH), lambda b, i: (b, jnp.maximum(i * nhalo - 1, 0), 0)),
        pl.BlockSpec((1, CONV_HALO, CONV_WIDTH),
                     lambda b, i: (b, jnp.minimum((i + 1) * nhalo, last_halo), 0)),
    ]
    args += [glu, glu, glu]
    for a in (conv_w, conv_b, ln_g, ln_b, w_out_bf, g2, wr, br, tri):
        in_specs.append(pl.BlockSpec(a.shape, lambda b, i: (0, 0)))
        args.append(a)
    return pl.pallas_call(
        _mix_kernel,
        grid=(B, S // tm),
        in_specs=in_specs,
        out_specs=[pl.BlockSpec((1, tm, D), lambda b, i: (b, i, 0)),
                   pl.BlockSpec((1, tm, D), lambda b, i: (b, i, 0)),
                   pl.BlockSpec((1, tm, LANES), lambda b, i: (b, i, 0)),
                   pl.BlockSpec((1, LANES), lambda b, i: (0, 0))],
        out_shape=[jax.ShapeDtypeStruct((B, S, D), F32),
                   jax.ShapeDtypeStruct((B, S, D), F32),
                   jax.ShapeDtypeStruct((B, S, LANES), F32),
                   jax.ShapeDtypeStruct((1, LANES), F32)],
        scratch_shapes=[pltpu.VMEM((tm, D), BF16),
                        pltpu.VMEM((4, GROUP_WIDTH // LANES, tm, LANES), F32),
                        pltpu.VMEM((CONV_WIDTH // LANES, tm + 2 * CONV_HALO, LANES), F32),
                        pltpu.VMEM((1, LANES), F32)],
        compiler_params=_params(2),
        name="mix_outproj_router",
    )(*args)


def _dispatch_copy(xp_ref, r, xs_hbm, dest_row, sem):
    return pltpu.make_async_copy(xp_ref.at[pl.ds(r, 1), :], xs_hbm.at[pl.ds(dest_row, 1), :], sem)


def _dispatch_kernel(dest_ref, xp_ref, xs_init_hbm, xs_hbm, sem):
    del xs_init_hbm
    rows = xp_ref.shape[0]
    base = pl.program_id(0) * rows

    def issue(r, carry):
        for k in range(2):
            _dispatch_copy(xp_ref, r, xs_hbm, dest_ref[2 * (base + r) + k], sem).start()
        return carry

    lax.fori_loop(0, rows, issue, 0, unroll=DMA_UNROLL)

    def drain(r, carry):
        for k in range(2):
            _dispatch_copy(xp_ref, r, xs_hbm, 0, sem).wait()
        return carry

    lax.fori_loop(0, rows, drain, 0, unroll=DMA_UNROLL)


def _dispatch_call(dest, xp, n_rows):
    T, W = xp.shape
    rows = DISPATCH_ROWS
    grid_spec = pltpu.PrefetchScalarGridSpec(
        num_scalar_prefetch=1,
        grid=(T // rows,),
        in_specs=[pl.BlockSpec((rows, W), lambda i, d: (i, 0)),
                  pl.BlockSpec(memory_space=pl.ANY)],
        out_specs=pl.BlockSpec(memory_space=pl.ANY),
        scratch_shapes=[pltpu.SemaphoreType.DMA],
    )
    return pl.pallas_call(
        _dispatch_kernel,
        grid_spec=grid_spec,
        out_shape=jax.ShapeDtypeStruct((n_rows, W), xp.dtype),
        input_output_aliases={2: 0},
        compiler_params=_params(1),
        name="moe_dispatch",
    )(dest, xp, jnp.zeros((n_rows, W), xp.dtype))


def _expert_kernel(be_ref, nu_ref, xs_ref, wg_ref, wu_ref, wd_ref, y_ref):
    del be_ref
    used = pl.program_id(0) < nu_ref[0]

    @pl.when(used)
    def _():
        xb = xs_ref[...].astype(BF16)
        hg = jnp.dot(xb, wg_ref[0], preferred_element_type=F32)
        hu = jnp.dot(xb, wu_ref[0], preferred_element_type=F32)
        hh =(hg * jax.nn.sigmoid(hg) * hu).astype(BF16)
        y_ref[...] = jnp.dot(hh, wd_ref[0], preferred_element_type=F32)

    @pl.when(jnp.logical_not(used))
    def _():
        y_ref[...] = jnp.zeros_like(y_ref)


def _expert_call(block_expert, n_used, xs, wg_bf, wu_bf, wd_bf):
    n_blocks = block_expert.shape[0]
    D = wg_bf.shape[1]

    def row_block(i, be, nu):
        return (jnp.minimum(i, nu[0] - 1), 0)

    def expert_block(i, be, nu):
        return (be[i], 0, 0)

    grid_spec = pltpu.PrefetchScalarGridSpec(
        num_scalar_prefetch=2,
        grid=(n_blocks,),
        in_specs=[
            pl.BlockSpec((MOE_ROWS, D), row_block),
            pl.BlockSpec((1, D, EXPERT_HIDDEN), expert_block),
            pl.BlockSpec((1, D, EXPERT_HIDDEN), expert_block),
            pl.BlockSpec((1, EXPERT_HIDDEN, D), expert_block),
        ],
        out_specs=pl.BlockSpec((MOE_ROWS, D), lambda i, be, nu: (i, 0)),
    )
    return pl.pallas_call(
        _expert_kernel,
        grid_spec=grid_spec,
        out_shape=jax.ShapeDtypeStruct((n_blocks * MOE_ROWS, D), F32),
        compiler_params=_params(1),
        name="moe_experts",
    )(block_expert, n_used, xs, wg_bf, wu_bf, wd_bf)


def _gather_copy(y_hbm, row, ybuf, slot, k, r, sems):
    return pltpu.make_async_copy(y_hbm.at[pl.ds(row, 1), :], ybuf.at[slot, k, pl.ds(r, 1), :], sems.at[slot])


def _combine_kernel(dest_ref, h_ref, route_ref, y_hbm, gf_ref, o_ref, ybuf, sems):
    rows = h_ref.shape[0]
    step = pl.program_id(0)
    n_steps = pl.num_programs(0)

    def issue(s, slot):
        def body(r, carry):
            for k in range(2):
                _gather_copy(y_hbm, dest_ref[2 * (s * rows + r) + k], ybuf, slot, k, r, sems).start()
            return carry
        lax.fori_loop(0, rows, body, 0, unroll=DMA_UNROLL)

    def drain(slot):
        def body(r, carry):
            for k in range(2):
                _gather_copy(y_hbm, 0, ybuf, slot, k, r, sems).wait()
            return carry
        lax.fori_loop(0, rows, body, 0, unroll=DMA_UNROLL)

    @pl.when(step == 0)
    def _():
        issue(0, 0)

    for slot in range(2):
        @pl.when((step % 2 == slot) & (step + 1 < n_steps))
        def _():
            issue(step + 1, 1 - slot)

    for slot in range(2):
        @pl.when(step % 2 == slot)
        def _():
            drain(slot)
            route = route_ref[...]
            z = h_ref[...] + (route[:, 4:5] * ybuf[slot, 0] + route[:, 5:6] * ybuf[slot, 1])
            o_ref[...] = z * lax.rsqrt(jnp.mean(z * z, axis=-1, keepdims=True) + NORM_EPS) * gf_ref[...]


def _combine_call(dest, h, route, y, gf):
    T, D = h.shape
    rows = COMBINE_ROWS
    grid_spec = pltpu.PrefetchScalarGridSpec(
        num_scalar_prefetch=1,
        grid=(T // rows,),
        in_specs=[
            pl.BlockSpec((rows, D), lambda i, d: (i, 0)),
            pl.BlockSpec((rows, LANES), lambda i, d: (i, 0)),
            pl.BlockSpec(memory_space=pl.ANY),
            pl.BlockSpec((1, D), lambda i, d: (0, 0)),
        ],
        out_specs=pl.BlockSpec((rows, D), lambda i, d: (i, 0)),
        scratch_shapes=[pltpu.VMEM((2, 2, rows, D), F32), pltpu.SemaphoreType.DMA((2,))],
    )
    return pl.pallas_call(
        _combine_kernel,
        grid_spec=grid_spec,
        out_shape=jax.ShapeDtypeStruct((T, D), F32),
        compiler_params=_params(1),
        name="moe_combine_norm",
    )(dest, h, route, y, gf)


def _rope_tables(S):
    half = HEAD_DIM // 2
    inv_freq = ROPE_THETA ** (-jnp.arange(half, dtype=F32) / half)
    ang = jnp.arange(S, dtype=F32)[:, None] * inv_freq[None, :]
    cos = jnp.cos(ang)
    sin = jnp.sin(ang)
    reps = LANES // HEAD_DIM
    return jnp.tile(cos, (1, 2 * reps)), jnp.tile(jnp.concatenate([-sin, sin], axis=1), (1, reps))


def _router_weights(w_rg, b_rg, w_re, b_re):
    D = w_re.shape[0]
    w = jnp.zeros((D, LANES), F32).at[:, :N_EXPERTS].set(w_re)
    w = w.at[:, N_EXPERTS:N_EXPERTS + N_EXPERT_GROUPS].set(w_rg)
    hi = w.astype(BF16)
    lo = (w - hi.astype(F32)).astype(BF16)
    b = jnp.zeros((1, LANES), F32).at[0, :N_EXPERTS].set(b_re)
    b = b.at[0, N_EXPERTS:N_EXPERTS + N_EXPERT_GROUPS].set(b_rg)
    return jnp.concatenate([hi, lo], axis=1), b


def _trunk(x, p):
    B, S, D = x.shape
    T = B * S
    cos_t, sin_t = _rope_tables(S)
    o0, o1, o2, glu = _inproj_call(x, p["g1"], p["w_in"], cos_t, sin_t)
    attn_out = []
    for d, qkv in zip(DILATIONS, (o0, o1, o2)):
        attn_out.append(_attn_call(qkv.reshape(3, B * d, S // d, GROUP_WIDTH)))
    h, xn, route, counts = _mix_call(x, attn_out, glu, p["conv_w"], p["conv_b"], p["ln_g"], p["ln_b"],
                                     p["w_out"], p["g2"], p["wr"], p["br"], p["tri"])
    h = h.reshape(T, D)
    xn = xn.reshape(T, D)
    route = route.reshape(T, LANES)

    eid = route[:, 0:2].astype(jnp.int32)
    rank = route[:, 2:4].astype(jnp.int32)
    cnt = counts[0, :N_EXPERTS].astype(jnp.int32)
    pcnt = (cnt + MOE_ROWS - 1) // MOE_ROWS * MOE_ROWS
    pend = jnp.cumsum(pcnt)
    pstart = pend - pcnt
    experts = jnp.arange(N_EXPERTS, dtype=jnp.int32)
    dest = (jnp.sum(jnp.where(eid[:, :, None] == experts, pstart, 0), axis=-1) + rank).reshape(2 * T)
    n_blocks = (2 * T) // MOE_ROWS + N_EXPERTS
    n_used = pend[-1:] // MOE_ROWS
    block_start = jnp.minimum(jnp.arange(n_blocks, dtype=jnp.int32), n_used - 1) * MOE_ROWS
    block_expert = jnp.sum(block_start[:, None] >= pend[None, :], axis=-1).astype(jnp.int32)

    xs = _dispatch_call(dest, xn, n_blocks * MOE_ROWS)
    y = _expert_call(block_expert, n_used.astype(jnp.int32), xs, p["wg"], p["wu"], p["wd"])
    out = _combine_call(dest, h, route, y, p["gf"])
    return out.reshape(B, S, D)


def kernel(x_prompt, x_sample, norm1_g, w_in, conv_dw_w, conv_dw_b, conv_ln_g, conv_ln_b, w_out, norm2_g,
           router_w_group, router_b_group, router_w_expert, router_b_expert, expert_w_gate, expert_w_up,
           expert_w_down, norm_f_g):
    assert norm1_g.shape[0] == 1, "one encoder layer"
    wr, br = _router_weights(router_w_group[0], router_b_group[0], router_w_expert[0], router_b_expert[0])
    tm = TOKEN_TILE
    tri = (lax.broadcasted_iota(jnp.int32, (tm, tm), 1)
           < lax.broadcasted_iota(jnp.int32, (tm, tm), 0)).astype(BF16)
    p = {
        "g1": norm1_g[0][None, :],
        "w_in": w_in[0].astype(BF16),
        "conv_w": jnp.pad(conv_dw_w[0], ((0, 1), (0, 0))),
        "conv_b": conv_dw_b[0][None, :],
        "ln_g": conv_ln_g[0][None, :],
        "ln_b": conv_ln_b[0][None, :],
        "w_out": w_out[0].astype(BF16),
        "g2": norm2_g[0][None, :],
        "wr": wr,
        "br": br,
        "tri": tri,
        "wg": expert_w_gate[0].astype(BF16),
        "wu": expert_w_up[0].astype(BF16),
        "wd": expert_w_down[0].astype(BF16),
        "gf": norm_f_g[None, :],
    }
    return (_trunk(x_prompt, p), _trunk(x_sample, p))
```

```python
import functools

import jax
import jax.numpy as jnp
from jax import lax
from jax.experimental import pallas as pl
from jax.experimental.pallas import tpu as pltpu

D_MODEL = 1024
HEAD_DIM = 64
HEADS_PER_GROUP = 4
GROUP_WIDTH = HEADS_PER_GROUP * HEAD_DIM
DILATIONS = (1, 4, 16)
HALF_WINDOW = 64
N_GROUPS = len(DILATIONS)
ATTN_WIDTH = N_GROUPS * GROUP_WIDTH
CONV_WIDTH = D_MODEL - ATTN_WIDTH
CONV_KERNEL = 31
CONV_HALO = 16
ROPE_THETA = 10000.0
N_EXPERT_GROUPS = 4
EXPERTS_PER_GROUP = 8
N_EXPERTS = N_EXPERT_GROUPS * EXPERTS_PER_GROUP
EXPERT_HIDDEN = D_MODEL // 2
NORM_EPS = 1e-6
NEG_INF = -1e30

LANES = 128
TOKEN_TILE = 512
ATTN_QBLOCK = 128
ATTN_KBLOCK = ATTN_QBLOCK + 2 * HALF_WINDOW
MOE_ROWS = 256
DISPATCH_ROWS = 512
COMBINE_ROWS = 256
DMA_UNROLL = 8
ROW_CHUNKS = D_MODEL // LANES
VMEM_LIMIT = 56 * 1024 * 1024

F32 = jnp.float32
BF16 = jnp.bfloat16


def _params(n_axes):
    return pltpu.CompilerParams(dimension_semantics=("arbitrary",) * n_axes,
                                vmem_limit_bytes=VMEM_LIMIT)


def _inproj_kernel(x_ref, g_ref, w_ref, cos_ref, sin_ref, o0_ref, o1_ref, o2_ref, glu_ref, scr_ref):
    tm = x_ref.shape[1]
    x = x_ref[0]
    u = x * lax.rsqrt(jnp.mean(x * x, axis=-1, keepdims=True) + NORM_EPS) * g_ref[...]
    ub = u.astype(BF16)
    cos = cos_ref[...]
    sin = sin_ref[...]
    lane = lax.broadcasted_iota(jnp.int32, (tm, LANES), 1)
    first_half = (lane % HEAD_DIM) < (HEAD_DIM // 2)

    def rope(t):
        partner = jnp.where(first_half, pltpu.roll(t, LANES - HEAD_DIM // 2, axis=1),
                            pltpu.roll(t, HEAD_DIM // 2, axis=1))
        return t * cos + partner * sin

    for part in range(3):
        p = jnp.dot(ub, w_ref[:, part * ATTN_WIDTH:(part + 1) * ATTN_WIDTH],
                    preferred_element_type=F32)
        for cb in range(ATTN_WIDTH // LANES):
            t = p[:, cb * LANES:(cb + 1) * LANES]
            if part == 0:
                t = rope(t) * (HEAD_DIM ** -0.5)
            elif part == 1:
                t = rope(t)
            scr_ref[part * (ATTN_WIDTH // LANES) + cb] = t
    c = jnp.dot(ub, w_ref[:, 3 * ATTN_WIDTH:], preferred_element_type=F32)
    glu_ref[0] = c[:, :CONV_WIDTH] * jax.nn.sigmoid(c[:, CONV_WIDTH:])

    lane_blocks = GROUP_WIDTH // LANES
    for part in range(3):
        for g, (d, o_ref) in enumerate(zip(DILATIONS, (o0_ref, o1_ref, o2_ref))):
            for lb in range(lane_blocks):
                cb = part * (ATTN_WIDTH // LANES) + g * lane_blocks + lb
                for c_res in range(d):
                    if d == 1:
                        rows = scr_ref[cb]
                    else:
                        rows = scr_ref[cb, pl.ds(c_res, tm // d, stride=d), :]
                    o_ref[part, 0, c_res, :, lb * LANES:(lb + 1) * LANES] = rows.astype(BF16)


def _inproj_call(x, g1, w_in_bf, cos_t, sin_t):
    B, S, D = x.shape
    tm = TOKEN_TILE
    grid = (B, S // tm)
    out_shape = [jax.ShapeDtypeStruct((3, B, d, S // d, GROUP_WIDTH), BF16) for d in DILATIONS]
    out_shape.append(jax.ShapeDtypeStruct((B, S, CONV_WIDTH), F32))
    out_specs = [pl.BlockSpec((3, 1, d, tm // d, GROUP_WIDTH), lambda b, i: (0, b, 0, i, 0))
                 for d in DILATIONS]
    out_specs.append(pl.BlockSpec((1, tm, CONV_WIDTH), lambda b, i: (b, i, 0)))
    return pl.pallas_call(
        _inproj_kernel,
        grid=grid,
        in_specs=[
            pl.BlockSpec((1, tm, D), lambda b, i: (b, i, 0)),
            pl.BlockSpec((1, D), lambda b, i: (0, 0)),
            pl.BlockSpec(w_in_bf.shape, lambda b, i: (0, 0)),
            pl.BlockSpec((tm, LANES), lambda b, i: (i, 0)),
            pl.BlockSpec((tm, LANES), lambda b, i: (i, 0)),
        ],
        out_specs=out_specs,
        out_shape=out_shape,
        scratch_shapes=[pltpu.VMEM((3 * ATTN_WIDTH // LANES, tm, LANES), F32)],
        compiler_params=_params(2),
        name="inproj",
    )(x, g1, w_in_bf, cos_t, sin_t)


def _attn_kernel(q_ref, k_ref, kl_ref, kr_ref, v_ref, vl_ref, vr_ref, o_ref, lse_ref, kx_ref, vx_ref,
                 *, seq_len):
    bq = q_ref.shape[2]
    hw = HALF_WINDOW
    q0 = pl.program_id(1) * bq
    kx_ref[0:hw] = kl_ref[0, 0]
    kx_ref[hw:hw + bq] = k_ref[0, 0]
    kx_ref[hw + bq:] = kr_ref[0, 0]
    vx_ref[0:hw] = vl_ref[0, 0]
    vx_ref[hw:hw + bq] = v_ref[0, 0]
    vx_ref[hw + bq:] = vr_ref[0, 0]

    nh = HEADS_PER_GROUP
    head_of_lane = lax.broadcasted_iota(jnp.int32, (ATTN_QBLOCK, GROUP_WIDTH), 1) // HEAD_DIM
    qq = lax.broadcasted_iota(jnp.int32, (ATTN_QBLOCK, ATTN_KBLOCK), 0)
    kk = lax.broadcasted_iota(jnp.int32, (ATTN_QBLOCK, ATTN_KBLOCK), 1)
    band = jnp.abs(kk - hw - qq) <= hw

    for j in range(bq // ATTN_QBLOCK):
        r0 = j * ATTN_QBLOCK
        qj = q_ref[0, 0, r0:r0 + ATTN_QBLOCK, :]
        kj = kx_ref[r0:r0 + ATTN_KBLOCK, :]
        vj = vx_ref[r0:r0 + ATTN_KBLOCK, :]
        kpos = kk + (q0 + r0 - hw)
        valid = band & (kpos >= 0) & (kpos < seq_len)
        qs = jnp.concatenate([jnp.where(head_of_lane == h, qj, jnp.zeros_like(qj)) for h in range(nh)],
                             axis=0)
        s = lax.dot_general(qs, kj, (((1,), (1,)), ((), ())), preferred_element_type=F32)
        s = jnp.where(jnp.concatenate([valid] * nh, axis=0), s, NEG_INF)
        m = jnp.max(s, axis=-1, keepdims=True)
        p = jnp.exp(s - m)
        den = jnp.sum(p, axis=-1, keepdims=True)
        r = jnp.dot(p.astype(BF16), vj, preferred_element_type=F32) / den
        lse = m + jnp.log(den)
        out = jnp.zeros((ATTN_QBLOCK, GROUP_WIDTH), F32)
        lse_full = jnp.zeros((ATTN_QBLOCK, GROUP_WIDTH), F32)
        for h in range(nh):
            sel = head_of_lane == h
            out = jnp.where(sel, r[h * ATTN_QBLOCK:(h + 1) * ATTN_QBLOCK], out)
            lse_full = jnp.where(sel, lse[h * ATTN_QBLOCK:(h + 1) * ATTN_QBLOCK], lse_full)
        o_ref[0, r0:r0 + ATTN_QBLOCK, :] = out.astype(BF16)
        lse_ref[0, r0:r0 + ATTN_QBLOCK, :] = lse_full


def _attn_call(qkv):
    _, nseq, L, W = qkv.shape
    bq = min(L, TOKEN_TILE)
    hw = HALF_WINDOW
    nhb = bq // hw
    last_hb = L // hw - 1

    def main(part):
        return pl.BlockSpec((1, 1, bq, W), lambda n, i: (part, n, i, 0))

    def left(part):
        return pl.BlockSpec((1, 1, hw, W), lambda n, i: (part, n, jnp.maximum(i * nhb - 1, 0), 0))

    def right(part):
        return pl.BlockSpec((1, 1, hw, W), lambda n, i: (part, n, jnp.minimum((i + 1) * nhb, last_hb), 0))

    return pl.pallas_call(
        functools.partial(_attn_kernel, seq_len=L),
        grid=(nseq, L // bq),
        in_specs=[main(0), main(1), left(1), right(1), main(2), left(2), right(2)],
        out_specs=[pl.BlockSpec((1, bq, W), lambda n, i: (n, i, 0)),
                   pl.BlockSpec((1, bq, W), lambda n, i: (n, i, 0))],
        out_shape=[jax.ShapeDtypeStruct((nseq, L, W), BF16),
                   jax.ShapeDtypeStruct((nseq, L, W), F32)],
        scratch_shapes=[pltpu.VMEM((bq + 2 * hw, W), BF16), pltpu.VMEM((bq + 2 * hw, W), BF16)],
        compiler_params=_params(2),
        name="band_attn",
    )(qkv, qkv, qkv, qkv, qkv, qkv, qkv)


def _mix_kernel(x_ref, o0_ref, l0_ref, o1_ref, l1_ref, o2_ref, l2_ref, glu_ref, glul_ref, glur_ref,
                cw_ref, cb_ref, lng_ref, lnb_ref, wo_ref, g2_ref, wr_ref, br_ref, tri_ref,
                h_ref, xn_ref, route_ref, cnt_ref,
                cat_ref, nat_ref, hpad_ref, carry_ref):
    tm = x_ref.shape[1]
    b = pl.program_id(0)
    i = pl.program_id(1)
    n_i = pl.num_programs(1)

    @pl.when((b == 0) & (i == 0))
    def _():
        carry_ref[...] = jnp.zeros_like(carry_ref)

    for slot, (src, d) in enumerate(((o1_ref, DILATIONS[1]), (l1_ref, DILATIONS[1]),
                                     (o2_ref, DILATIONS[2]), (l2_ref, DILATIONS[2]))):
        for c_res in range(d):
            for lb in range(GROUP_WIDTH // LANES):
                nat_ref[slot, lb, pl.ds(c_res, tm // d, stride=d), :] = (
                    src[0, c_res, :, lb * LANES:(lb + 1) * LANES].astype(F32))

    def natural(slot):
        return jnp.concatenate([nat_ref[slot, lb] for lb in range(GROUP_WIDTH // LANES)], axis=1)

    outs = (o0_ref[0, 0].astype(F32), natural(0), natural(2))
    lses = (l0_ref[0, 0], natural(1), natural(3))
    lmax = jnp.maximum(jnp.maximum(lses[0], lses[1]), lses[2])
    es = [jnp.exp(l - lmax) for l in lses]
    inv = 1.0 / (es[0] + es[1] + es[2])
    for g in range(N_GROUPS):
        cat_ref[:, g * GROUP_WIDTH:(g + 1) * GROUP_WIDTH] = (outs[g] * (es[g] * inv)).astype(BF16)

    zeros_halo = jnp.zeros((CONV_HALO, CONV_WIDTH), F32)
    left = jnp.where(i > 0, glul_ref[0], zeros_halo)
    right = jnp.where(i < n_i - 1, glur_ref[0], zeros_halo)
    conv_blocks = CONV_WIDTH // LANES
    for cb in range(conv_blocks):
        cols = slice(cb * LANES, (cb + 1) * LANES)
        hpad_ref[cb, 0:CONV_HALO] = left[:, cols]
        hpad_ref[cb, CONV_HALO:CONV_HALO + tm] = glu_ref[0, :, cols]
        hpad_ref[cb, CONV_HALO + tm:] = right[:, cols]
    chunk = 64
    off0 = CONV_HALO - CONV_KERNEL // 2
    for r0 in range(0, tm, chunk):
        accs = []
        for cb in range(conv_blocks):
            cols = slice(cb * LANES, (cb + 1) * LANES)
            a = jnp.zeros((chunk, LANES), F32) + cb_ref[:, cols]
            for j in range(CONV_KERNEL):
                a = a + hpad_ref[cb, r0 + off0 + j: r0 + off0 + j + chunk, :] * cw_ref[j:j + 1, cols]
            accs.append(a)
        acc = jnp.concatenate(accs, axis=1)
        mu = jnp.mean(acc, axis=-1, keepdims=True)
        xc = acc - mu
        var = jnp.mean(xc * xc, axis=-1, keepdims=True)
        y = xc * lax.rsqrt(var + NORM_EPS) * lng_ref[...] + lnb_ref[...]
        cat_ref[r0:r0 + chunk, ATTN_WIDTH:] = (y * jax.nn.sigmoid(y)).astype(BF16)

    h = x_ref[0] + jnp.dot(cat_ref[...], wo_ref[...], preferred_element_type=F32)
    h_ref[0] = h
    xn = h * lax.rsqrt(jnp.mean(h * h, axis=-1, keepdims=True) + NORM_EPS) * g2_ref[...]

    for c in range(ROW_CHUNKS):
        xn_ref[0, pl.ds(c, tm, stride=ROW_CHUNKS), :] = xn[:, c * LANES:(c + 1) * LANES]
    xh = xn.astype(BF16)
    xl = (xn - xh.astype(F32)).astype(BF16)
    rr = jnp.dot(jnp.concatenate([xh, xl], axis=0), wr_ref[...], preferred_element_type=F32)
    lg = (rr[:tm, :LANES] + rr[:tm, LANES:]) + (rr[tm:, :LANES] + rr[tm:, LANES:]) + br_ref[...]

    lane = lax.broadcasted_iota(jnp.int32, (tm, LANES), 1)
    lanef = lane.astype(F32)
    big = jnp.float32(1e9)
    is_g = (lane >= N_EXPERTS) & (lane < N_EXPERTS + N_EXPERT_GROUPS)
    gl = jnp.where(is_g, lg, -jnp.inf)
    gmax = jnp.max(gl, axis=-1, keepdims=True)
    grp = jnp.min(jnp.where(gl == gmax, lanef, big), axis=-1, keepdims=True) - N_EXPERTS
    p_grp = 1.0 / jnp.sum(jnp.where(is_g, jnp.exp(lg - gmax), 0.0), axis=-1, keepdims=True)
    in_grp = (lanef >= grp * EXPERTS_PER_GROUP) & (lanef < (grp + 1) * EXPERTS_PER_GROUP)
    el = jnp.where(in_grp, lg, -jnp.inf)
    v1 = jnp.max(el, axis=-1, keepdims=True)
    i1 = jnp.min(jnp.where(el == v1, lanef, big), axis=-1, keepdims=True)
    el2 = jnp.where(lanef == i1, -jnp.inf, el)
    v2 = jnp.max(el2, axis=-1, keepdims=True)
    i2 = jnp.min(jnp.where(el2 == v2, lanef, big), axis=-1, keepdims=True)
    t = jnp.exp(v2 - v1)
    w1 = p_grp / (1.0 + t)
    w2 = p_grp * t / (1.0 + t)

    hit1 = lanef == i1
    hit2 = lanef == i2
    onehot = jnp.where(hit1 | hit2, 1.0, 0.0)
    tot = carry_ref[...] + jnp.dot(tri_ref[...], onehot.astype(BF16), preferred_element_type=F32)
    r1 = jnp.sum(jnp.where(hit1, tot, 0.0), axis=-1, keepdims=True)
    r2 = jnp.sum(jnp.where(hit2, tot, 0.0), axis=-1, keepdims=True)
    carry_ref[...] = carry_ref[...] + jnp.sum(onehot, axis=0, keepdims=True)
    cnt_ref[...] = carry_ref[...]

    route = jnp.where(lane == 0, i1, 0.0)
    route = jnp.where(lane == 1, i2, route)
    route = jnp.where(lane == 2, r1, route)
    route = jnp.where(lane == 3, r2, route)
    route = jnp.where(lane == 4, w1, route)
    route = jnp.where(lane == 5, w2, route)
    route_ref[0] = route


def _mix_call(x, attn_out, glu, conv_w, conv_b, ln_g, ln_b, w_out_bf, g2, wr, br, tri):
    B, S, D = x.shape
    tm = TOKEN_TILE
    nhalo = tm // CONV_HALO
    last_halo = S // CONV_HALO - 1
    in_specs = [pl.BlockSpec((1, tm, D), lambda b, i: (b, i, 0))]
    args = [x]
    for d, (o, l) in zip(DILATIONS, attn_out):
        spec = pl.BlockSpec((1, d, tm // d, GROUP_WIDTH), lambda b, i: (b, 0, i, 0))
        in_specs += [spec, spec]
        args += [o.reshape(B, d, S // d, GROUP_WIDTH), l.reshape(B, d, S // d, GROUP_WIDTH)]
    in_specs += [
        pl.BlockSpec((1, tm, CONV_WIDTH), lambda b, i: (b, i, 0)),
        pl.BlockSpec((1, CONV_HALO, CONV_WIDTH), lambda b, i: (b, jnp.maximum(i * nhalo - 1, 0), 0)),
        pl.BlockSpec((1, CONV_HALO, CONV_WIDTH),
                     lambda b, i: (b, jnp.minimum((i + 1) * nhalo, last_halo), 0)),
    ]
    args += [glu, glu, glu]
    for a in (conv_w, conv_b, ln_g, ln_b, w_out_bf, g2, wr, br, tri):
        in_specs.append(pl.BlockSpec(a.shape, lambda b, i: (0, 0)))
        args.append(a)
    return pl.pallas_call(
        _mix_kernel,
        grid=(B, S // tm),
        in_specs=in_specs,
        out_specs=[pl.BlockSpec((1, tm, D), lambda b, i: (b, i, 0)),
                   pl.BlockSpec((1, tm * ROW_CHUNKS, LANES), lambda b, i: (b, i, 0)),
                   pl.BlockSpec((1, tm, LANES), lambda b, i: (b, i, 0)),
                   pl.BlockSpec((1, LANES), lambda b, i: (0, 0))],
        out_shape=[jax.ShapeDtypeStruct((B, S, D), F32),
                   jax.ShapeDtypeStruct((B, S * ROW_CHUNKS, LANES), F32),
                   jax.ShapeDtypeStruct((B, S, LANES), F32),
                   jax.ShapeDtypeStruct((1, LANES), F32)],
        scratch_shapes=[pltpu.VMEM((tm, D), BF16),
                        pltpu.VMEM((4, GROUP_WIDTH // LANES, tm, LANES), F32),
                        pltpu.VMEM((CONV_WIDTH // LANES, tm + 2 * CONV_HALO, LANES), F32),
                        pltpu.VMEM((1, LANES), F32)],
        compiler_params=_params(2),
        name="mix_outproj_router",
    )(*args)


def _tile_rows(ref, tile):
    return ref.at[pl.ds(pl.multiple_of(tile * ROW_CHUNKS, ROW_CHUNKS), ROW_CHUNKS), :]


def _dispatch_copy(xn_ref, r, xs_hbm, dest_row, sem):
    return pltpu.make_async_copy(_tile_rows(xn_ref, r), _tile_rows(xs_hbm, dest_row), sem)


def _zero_fill_copy(zero_ref, xs_hbm, end_row, sem):
    start = pl.multiple_of((end_row - MOE_ROWS) * ROW_CHUNKS, ROW_CHUNKS)
    return pltpu.make_async_copy(zero_ref, xs_hbm.at[pl.ds(start, MOE_ROWS * ROW_CHUNKS), :], sem)


def _dispatch_kernel(dest_ref, pend_ref, xn_ref, xs_hbm, zero_ref, sem, zsem):
    rows = xn_ref.shape[0] // ROW_CHUNKS
    base = pl.program_id(0) * rows

    @pl.when(pl.program_id(0) == 0)
    def _():
        zero_ref[...] = jnp.zeros_like(zero_ref)
        n_rows = xs_hbm.shape[0] // ROW_CHUNKS
        for wait in (False, True):
            for e in range(N_EXPERTS):
                prev = pend_ref[e - 1] if e else 0

                @pl.when(pend_ref[e] > prev)
                def _():
                    copy = _zero_fill_copy(zero_ref, xs_hbm, pend_ref[e], zsem)
                    copy.wait() if wait else copy.start()

            for j in range(1, N_EXPERTS + 1):
                end_row = pend_ref[N_EXPERTS - 1] + j * MOE_ROWS

                @pl.when(end_row <= n_rows)
                def _():
                    copy = _zero_fill_copy(zero_ref, xs_hbm, end_row, zsem)
                    copy.wait() if wait else copy.start()

    def issue(r, carry):
        for k in range(2):
            _dispatch_copy(xn_ref, r, xs_hbm, dest_ref[2 * (base + r) + k], sem).start(priority=k)
        return carry

    lax.fori_loop(0, rows, issue, 0, unroll=DMA_UNROLL)

    def drain(r, carry):
        for k in range(2):
            _dispatch_copy(xn_ref, r, xs_hbm, 0, sem).wait()
        return carry

    lax.fori_loop(0, rows, drain, 0, unroll=DMA_UNROLL)


def _dispatch_call(dest, pend, xn_tiles, n_rows):
    rows = DISPATCH_ROWS
    n_tok = xn_tiles.shape[0] // ROW_CHUNKS
    grid_spec = pltpu.PrefetchScalarGridSpec(
        num_scalar_prefetch=2,
        grid=(n_tok // rows,),
        in_specs=[pl.BlockSpec((rows * ROW_CHUNKS, LANES), lambda i, d, pe: (i, 0))],
        out_specs=pl.BlockSpec(memory_space=pl.ANY),
        scratch_shapes=[pltpu.VMEM((MOE_ROWS * ROW_CHUNKS, LANES), F32),
                        pltpu.SemaphoreType.DMA, pltpu.SemaphoreType.DMA],
    )
    return pl.pallas_call(
        _dispatch_kernel,
        grid_spec=grid_spec,
        out_shape=jax.ShapeDtypeStruct((n_rows * ROW_CHUNKS, LANES), F32),
        compiler_params=_params(1),
        name="moe_dispatch",
    )(dest, pend, xn_tiles)


def _expert_kernel(be_ref, nu_ref, xs_ref, wg_ref, wu_ref, wd_ref, y_ref, xb_ref):
    del be_ref
    used = pl.program_id(0) < nu_ref[0]

    @pl.when(used)
    def _():
        for c in range(ROW_CHUNKS):
            xb_ref[:, c * LANES:(c + 1) * LANES] = (
                xs_ref[pl.ds(c, MOE_ROWS, stride=ROW_CHUNKS), :].astype(BF16))
        xb = xb_ref[...]
        hg = jnp.dot(xb, wg_ref[0], preferred_element_type=F32)
        hu = jnp.dot(xb, wu_ref[0], preferred_element_type=F32)
        hh = (hg * jax.nn.sigmoid(hg) * hu).astype(BF16)
        y = jnp.dot(hh, wd_ref[0], preferred_element_type=F32)
        for c in range(ROW_CHUNKS):
            y_ref[pl.ds(c, MOE_ROWS, stride=ROW_CHUNKS), :] = y[:, c * LANES:(c + 1) * LANES]

    @pl.when(jnp.logical_not(used))
    def _():
        y_ref[...] = jnp.zeros_like(y_ref)


def _expert_call(block_expert, n_used, xs_tiles, wg_bf, wu_bf, wd_bf):
    n_blocks = block_expert.shape[0]
    D = wg_bf.shape[1]
    tile_rows = MOE_ROWS * ROW_CHUNKS

    def row_block(i, be, nu):
        return (jnp.minimum(i, nu[0] - 1), 0)

    def expert_block(i, be, nu):
        return (be[i], 0, 0)

    grid_spec = pltpu.PrefetchScalarGridSpec(
        num_scalar_prefetch=2,
        grid=(n_blocks,),
        in_specs=[
            pl.BlockSpec((tile_rows, LANES), row_block),
            pl.BlockSpec((1, D, EXPERT_HIDDEN), expert_block),
            pl.BlockSpec((1, D, EXPERT_HIDDEN), expert_block),
            pl.BlockSpec((1, EXPERT_HIDDEN, D), expert_block),
        ],
        out_specs=pl.BlockSpec((tile_rows, LANES), lambda i, be, nu: (i, 0)),
        scratch_shapes=[pltpu.VMEM((MOE_ROWS, D), BF16)],
    )
    return pl.pallas_call(
        _expert_kernel,
        grid_spec=grid_spec,
        out_shape=jax.ShapeDtypeStruct((n_blocks * tile_rows, LANES), F32),
        compiler_params=_params(1),
        name="moe_experts",
    )(block_expert, n_used, xs_tiles, wg_bf, wu_bf, wd_bf)


def _gather_copy(y_hbm, row, ybuf, slot, k, r, sems):
    return pltpu.make_async_copy(_tile_rows(y_hbm, row), _tile_rows(ybuf.at[slot, k], r), sems.at[slot])


def _combine_kernel(dest_ref, h_ref, route_ref, y_hbm, gf_ref, o_ref, ybuf, ynat, sems):
    rows = h_ref.shape[0]
    step = pl.program_id(0)
    n_steps = pl.num_programs(0)

    def issue(s, slot):
        def body(r, carry):
            for k in range(2):
                _gather_copy(y_hbm, dest_ref[2 * (s * rows + r) + k], ybuf, slot, k, r, sems).start(priority=k)
            return carry
        lax.fori_loop(0, rows, body, 0, unroll=DMA_UNROLL)

    def drain(slot):
        def body(r, carry):
            for k in range(2):
                _gather_copy(y_hbm, 0, ybuf, slot, k, r, sems).wait()
            return carry
        lax.fori_loop(0, rows, body, 0, unroll=DMA_UNROLL)

    @pl.when(step == 0)
    def _():
        issue(0, 0)

    for slot in range(2):
        @pl.when((step % 2 == slot) & (step + 1 < n_steps))
        def _():
            issue(step + 1, 1 - slot)

    for slot in range(2):
        @pl.when(step % 2 == slot)
        def _():
            drain(slot)
            for k in range(2):
                for c in range(ROW_CHUNKS):
                    ynat[k, :, c * LANES:(c + 1) * LANES] = ybuf[slot, k, pl.ds(c, rows, stride=ROW_CHUNKS), :]
            route = route_ref[...]
            z = h_ref[...] + (route[:, 4:5] * ynat[0] + route[:, 5:6] * ynat[1])
            o_ref[...] = z * lax.rsqrt(jnp.mean(z * z, axis=-1, keepdims=True) + NORM_EPS) * gf_ref[...]


def _combine_call(dest, h, route, y, gf):
    T, D = h.shape
    rows = COMBINE_ROWS
    grid_spec = pltpu.PrefetchScalarGridSpec(
        num_scalar_prefetch=1,
        grid=(T // rows,),
        in_specs=[
            pl.BlockSpec((rows, D), lambda i, d: (i, 0)),
            pl.BlockSpec((rows, LANES), lambda i, d: (i, 0)),
            pl.BlockSpec(memory_space=pl.ANY),
            pl.BlockSpec((1, D), lambda i, d: (0, 0)),
        ],
        out_specs=pl.BlockSpec((rows, D), lambda i, d: (i, 0)),
        scratch_shapes=[pltpu.VMEM((2, 2, rows * ROW_CHUNKS, LANES), F32), pltpu.VMEM((2, rows, D), F32),
                        pltpu.SemaphoreType.DMA((2,))],
    )
    return pl.pallas_call(
        _combine_kernel,
        grid_spec=grid_spec,
        out_shape=jax.ShapeDtypeStruct((T, D), F32),
        compiler_params=_params(1),
        name="moe_combine_norm",
    )(dest, h, route, y, gf)


def _rope_tables(S):
    half = HEAD_DIM // 2
    inv_freq = ROPE_THETA ** (-jnp.arange(half, dtype=F32) / half)
    ang = jnp.arange(S, dtype=F32)[:, None] * inv_freq[None, :]
    cos = jnp.cos(ang)
    sin = jnp.sin(ang)
    reps = LANES // HEAD_DIM
    return jnp.tile(cos, (1, 2 * reps)), jnp.tile(jnp.concatenate([-sin, sin], axis=1), (1, reps))


def _router_weights(w_rg, b_rg, w_re, b_re):
    D = w_re.shape[0]
    w = jnp.zeros((D, LANES), F32).at[:, :N_EXPERTS].set(w_re)
    w = w.at[:, N_EXPERTS:N_EXPERTS + N_EXPERT_GROUPS].set(w_rg)
    hi = w.astype(BF16)
    lo = (w - hi.astype(F32)).astype(BF16)
    b = jnp.zeros((1, LANES), F32).at[0, :N_EXPERTS].set(b_re)
    b = b.at[0, N_EXPERTS:N_EXPERTS + N_EXPERT_GROUPS].set(b_rg)
    return jnp.concatenate([hi, lo], axis=1), b


def _trunk(x, p):
    B, S, D = x.shape
    T = B * S
    cos_t, sin_t = _rope_tables(S)
    o0, o1, o2, glu = _inproj_call(x, p["g1"], p["w_in"], cos_t, sin_t)
    attn_out = []
    for d, qkv in zip(DILATIONS, (o0, o1, o2)):
        attn_out.append(_attn_call(qkv.reshape(3, B * d, S // d, GROUP_WIDTH)))
    h, xn, route, counts = _mix_call(x, attn_out, glu, p["conv_w"], p["conv_b"], p["ln_g"], p["ln_b"],
                                     p["w_out"], p["g2"], p["wr"], p["br"], p["tri"])
    h = h.reshape(T, D)
    xn = xn.reshape(T * ROW_CHUNKS, LANES)
    route = route.reshape(T, LANES)

    eid = route[:, 0:2].astype(jnp.int32)
    rank = route[:, 2:4].astype(jnp.int32)
    cnt = counts[0, :N_EXPERTS].astype(jnp.int32)
    pcnt = (cnt + MOE_ROWS - 1) // MOE_ROWS * MOE_ROWS
    pend = jnp.cumsum(pcnt)
    pstart = pend - pcnt
    experts = jnp.arange(N_EXPERTS, dtype=jnp.int32)
    dest = (jnp.sum(jnp.where(eid[:, :, None] == experts, pstart, 0), axis=-1) + rank).reshape(2 * T)
    n_blocks = (2 * T) // MOE_ROWS + N_EXPERTS
    n_used = pend[-1:] // MOE_ROWS
    block_start = jnp.minimum(jnp.arange(n_blocks, dtype=jnp.int32), n_used - 1) * MOE_ROWS
    block_expert = jnp.sum(block_start[:, None] >= pend[None, :], axis=-1).astype(jnp.int32)

    xs = _dispatch_call(dest, pend.astype(jnp.int32), xn, n_blocks * MOE_ROWS)
    y = _expert_call(block_expert, n_used.astype(jnp.int32), xs, p["wg"], p["wu"], p["wd"])
    out = _combine_call(dest, h, route, y, p["gf"])
    return out.reshape(B, S, D)


def kernel(x_prompt, x_sample, norm1_g, w_in, conv_dw_w, conv_dw_b, conv_ln_g, conv_ln_b, w_out, norm2_g,
           router_w_group, router_b_group, router_w_expert, router_b_expert, expert_w_gate, expert_w_up,
           expert_w_down, norm_f_g):
    assert norm1_g.shape[0] == 1, "one encoder layer"
    wr, br = _router_weights(router_w_group[0], router_b_group[0], router_w_expert[0], router_b_expert[0])
    tm = TOKEN_TILE
    tri = (lax.broadcasted_iota(jnp.int32, (tm, tm), 1)
           < lax.broadcasted_iota(jnp.int32, (tm, tm), 0)).astype(BF16)
    p = {
        "g1": norm1_g[0][None, :],
        "w_in": w_in[0].astype(BF16),
        "conv_w": jnp.pad(conv_dw_w[0], ((0, 1), (0, 0))),
        "conv_b": conv_dw_b[0][None, :],
        "ln_g": conv_ln_g[0][None, :],
        "ln_b": conv_ln_b[0][None, :],
        "w_out": w_out[0].astype(BF16),
        "g2": norm2_g[0][None, :],
        "wr": wr,
        "br": br,
        "tri": tri,
        "wg": expert_w_gate[0].astype(BF16),
        "wu": expert_w_up[0].astype(BF16),
        "wd": expert_w_down[0].astype(BF16),
        "gf": norm_f_g[None, :],
    }
    return (_trunk(x_prompt, p), _trunk(x_sample, p))
```

```python
import functools

import jax
import jax.numpy as jnp
from jax import lax
from jax.experimental import pallas as pl
from jax.experimental.pallas import tpu as pltpu

D_MODEL = 1024
HEAD_DIM = 64
HEADS_PER_GROUP = 4
GROUP_WIDTH = HEADS_PER_GROUP * HEAD_DIM
DILATIONS = (1, 4, 16)
HALF_WINDOW = 64
N_GROUPS = len(DILATIONS)
ATTN_WIDTH = N_GROUPS * GROUP_WIDTH
CONV_WIDTH = D_MODEL - ATTN_WIDTH
CONV_KERNEL = 31
CONV_HALO = 16
ROPE_THETA = 10000.0
N_EXPERT_GROUPS = 4
EXPERTS_PER_GROUP = 8
N_EXPERTS = N_EXPERT_GROUPS * EXPERTS_PER_GROUP
EXPERT_HIDDEN = D_MODEL // 2
NORM_EPS = 1e-6
NEG_INF = -1e30

LANES = 128
TOKEN_TILE = 512
ATTN_QBLOCK = 128
ATTN_KBLOCK = ATTN_QBLOCK + 2 * HALF_WINDOW
MOE_ROWS = 256
DISPATCH_ROWS = 1024
COMBINE_ROWS = 512
DMA_UNROLL = 8
ROW_CHUNKS = D_MODEL // LANES
VMEM_LIMIT = 56 * 1024 * 1024

F32 = jnp.float32
BF16 = jnp.bfloat16


def _params(n_axes):
    return pltpu.CompilerParams(dimension_semantics=("arbitrary",) * n_axes,
                                vmem_limit_bytes=VMEM_LIMIT)


def _inproj_kernel(x_ref, g_ref, w_ref, cos_ref, sin_ref, o0_ref, o1_ref, o2_ref, glu_ref, scr_ref):
    tm = x_ref.shape[1]
    x = x_ref[0]
    u = x * lax.rsqrt(jnp.mean(x * x, axis=-1, keepdims=True) + NORM_EPS) * g_ref[...]
    ub = u.astype(BF16)
    cos = cos_ref[...]
    sin = sin_ref[...]
    lane = lax.broadcasted_iota(jnp.int32, (tm, LANES), 1)
    first_half = (lane % HEAD_DIM) < (HEAD_DIM // 2)

    def rope(t):
        partner = jnp.where(first_half, pltpu.roll(t, LANES - HEAD_DIM // 2, axis=1),
                            pltpu.roll(t, HEAD_DIM // 2, axis=1))
        return t * cos + partner * sin

    for part in range(3):
        p = jnp.dot(ub, w_ref[:, part * ATTN_WIDTH:(part + 1) * ATTN_WIDTH],
                    preferred_element_type=F32)
        for cb in range(ATTN_WIDTH // LANES):
            t = p[:, cb * LANES:(cb + 1) * LANES]
            if part == 0:
                t = rope(t) * (HEAD_DIM ** -0.5)
            elif part == 1:
                t = rope(t)
            scr_ref[part * (ATTN_WIDTH // LANES) + cb] = t
    c = jnp.dot(ub, w_ref[:, 3 * ATTN_WIDTH:], preferred_element_type=F32)
    glu_ref[0] = c[:, :CONV_WIDTH] * jax.nn.sigmoid(c[:, CONV_WIDTH:])

    lane_blocks = GROUP_WIDTH // LANES
    for part in range(3):
        for g, (d, o_ref) in enumerate(zip(DILATIONS, (o0_ref, o1_ref, o2_ref))):
            for lb in range(lane_blocks):
                cb = part * (ATTN_WIDTH // LANES) + g * lane_blocks + lb
                for c_res in range(d):
                    if d == 1:
                        rows = scr_ref[cb]
                    else:
                        rows = scr_ref[cb, pl.ds(c_res, tm // d, stride=d), :]
                    o_ref[part, 0, c_res, :, lb * LANES:(lb + 1) * LANES] = rows.astype(BF16)


def _inproj_call(x, g1, w_in_bf, cos_t, sin_t):
    B, S, D = x.shape
    tm = TOKEN_TILE
    grid = (B, S // tm)
    out_shape = [jax.ShapeDtypeStruct((3, B, d, S // d, GROUP_WIDTH), BF16) for d in DILATIONS]
    out_shape.append(jax.ShapeDtypeStruct((B, S, CONV_WIDTH), F32))
    out_specs = [pl.BlockSpec((3, 1, d, tm // d, GROUP_WIDTH), lambda b, i: (0, b, 0, i, 0))
                 for d in DILATIONS]
    out_specs.append(pl.BlockSpec((1, tm, CONV_WIDTH), lambda b, i: (b, i, 0)))
    return pl.pallas_call(
        _inproj_kernel,
        grid=grid,
        in_specs=[
            pl.BlockSpec((1, tm, D), lambda b, i: (b, i, 0)),
            pl.BlockSpec((1, D), lambda b, i: (0, 0)),
            pl.BlockSpec(w_in_bf.shape, lambda b, i: (0, 0)),
            pl.BlockSpec((tm, LANES), lambda b, i: (i, 0)),
            pl.BlockSpec((tm, LANES), lambda b, i: (i, 0)),
        ],
        out_specs=out_specs,
        out_shape=out_shape,
        scratch_shapes=[pltpu.VMEM((3 * ATTN_WIDTH // LANES, tm, LANES), F32)],
        compiler_params=_params(2),
        name="inproj",
    )(x, g1, w_in_bf, cos_t, sin_t)


def _attn_kernel(q_ref, k_ref, kl_ref, kr_ref, v_ref, vl_ref, vr_ref, o_ref, lse_ref, kx_ref, vx_ref,
                 *, seq_len):
    bq = q_ref.shape[2]
    hw = HALF_WINDOW
    q0 = pl.program_id(1) * bq
    kx_ref[0:hw] = kl_ref[0, 0]
    kx_ref[hw:hw + bq] = k_ref[0, 0]
    kx_ref[hw + bq:] = kr_ref[0, 0]
    vx_ref[0:hw] = vl_ref[0, 0]
    vx_ref[hw:hw + bq] = v_ref[0, 0]
    vx_ref[hw + bq:] = vr_ref[0, 0]

    nh = HEADS_PER_GROUP
    head_of_lane = lax.broadcasted_iota(jnp.int32, (ATTN_QBLOCK, GROUP_WIDTH), 1) // HEAD_DIM
    qq = lax.broadcasted_iota(jnp.int32, (ATTN_QBLOCK, ATTN_KBLOCK), 0)
    kk = lax.broadcasted_iota(jnp.int32, (ATTN_QBLOCK, ATTN_KBLOCK), 1)
    band = jnp.abs(kk - hw - qq) <= hw

    for j in range(bq // ATTN_QBLOCK):
        r0 = j * ATTN_QBLOCK
        qj = q_ref[0, 0, r0:r0 + ATTN_QBLOCK, :]
        kj = kx_ref[r0:r0 + ATTN_KBLOCK, :]
        vj = vx_ref[r0:r0 + ATTN_KBLOCK, :]
        kpos = kk + (q0 + r0 - hw)
        valid = band & (kpos >= 0) & (kpos < seq_len)
        qs = jnp.concatenate([jnp.where(head_of_lane == h, qj, jnp.zeros_like(qj)) for h in range(nh)],
                             axis=0)
        s = lax.dot_general(qs, kj, (((1,), (1,)), ((), ())), preferred_element_type=F32)
        s = jnp.where(jnp.concatenate([valid] * nh, axis=0), s, NEG_INF)
        m = jnp.max(s, axis=-1, keepdims=True)
        p = jnp.exp(s - m)
        den = jnp.sum(p, axis=-1, keepdims=True)
        r = jnp.dot(p.astype(BF16), vj, preferred_element_type=F32) / den
        lse = m + jnp.log(den)
        out = jnp.zeros((ATTN_QBLOCK, GROUP_WIDTH), F32)
        lse_full = jnp.zeros((ATTN_QBLOCK, GROUP_WIDTH), F32)
        for h in range(nh):
            sel = head_of_lane == h
            out = jnp.where(sel, r[h * ATTN_QBLOCK:(h + 1) * ATTN_QBLOCK], out)
            lse_full = jnp.where(sel, lse[h * ATTN_QBLOCK:(h + 1) * ATTN_QBLOCK], lse_full)
        o_ref[0, r0:r0 + ATTN_QBLOCK, :] = out.astype(BF16)
        lse_ref[0, r0:r0 + ATTN_QBLOCK, :] = lse_full


def _attn_call(qkv):
    _, nseq, L, W = qkv.shape
    bq = min(L, TOKEN_TILE)
    hw = HALF_WINDOW
    nhb = bq // hw
    last_hb = L // hw - 1

    def main(part):
        return pl.BlockSpec((1, 1, bq, W), lambda n, i: (part, n, i, 0))

    def left(part):
        return pl.BlockSpec((1, 1, hw, W), lambda n, i: (part, n, jnp.maximum(i * nhb - 1, 0), 0))

    def right(part):
        return pl.BlockSpec((1, 1, hw, W), lambda n, i: (part, n, jnp.minimum((i + 1) * nhb, last_hb), 0))

    return pl.pallas_call(
        functools.partial(_attn_kernel, seq_len=L),
        grid=(nseq, L // bq),
        in_specs=[main(0), main(1), left(1), right(1), main(2), left(2), right(2)],
        out_specs=[pl.BlockSpec((1, bq, W), lambda n, i: (n, i, 0)),
                   pl.BlockSpec((1, bq, W), lambda n, i: (n, i, 0))],
        out_shape=[jax.ShapeDtypeStruct((nseq, L, W), BF16),
                   jax.ShapeDtypeStruct((nseq, L, W), F32)],
        scratch_shapes=[pltpu.VMEM((bq + 2 * hw, W), BF16), pltpu.VMEM((bq + 2 * hw, W), BF16)],
        compiler_params=_params(2),
        name="band_attn",
    )(qkv, qkv, qkv, qkv, qkv, qkv, qkv)


def _mix_kernel(x_ref, o0_ref, l0_ref, o1_ref, l1_ref, o2_ref, l2_ref, glu_ref, glul_ref, glur_ref,
                cw_ref, cb_ref, lng_ref, lnb_ref, wo_ref, g2_ref, wr_ref, br_ref, tri_ref,
                h_ref, xn_ref, route_ref, cnt_ref,
                cat_ref, nat_ref, hpad_ref, carry_ref):
    tm = x_ref.shape[1]
    b = pl.program_id(0)
    i = pl.program_id(1)
    n_i = pl.num_programs(1)

    @pl.when((b == 0) & (i == 0))
    def _():
        carry_ref[...] = jnp.zeros_like(carry_ref)

    for slot, (src, d) in enumerate(((o1_ref, DILATIONS[1]), (l1_ref, DILATIONS[1]),
                                     (o2_ref, DILATIONS[2]), (l2_ref, DILATIONS[2]))):
        for c_res in range(d):
            for lb in range(GROUP_WIDTH // LANES):
                nat_ref[slot, lb, pl.ds(c_res, tm // d, stride=d), :] = (
                    src[0, c_res, :, lb * LANES:(lb + 1) * LANES].astype(F32))

    def natural(slot):
        return jnp.concatenate([nat_ref[slot, lb] for lb in range(GROUP_WIDTH // LANES)], axis=1)

    outs = (o0_ref[0, 0].astype(F32), natural(0), natural(2))
    lses = (l0_ref[0, 0], natural(1), natural(3))
    lmax = jnp.maximum(jnp.maximum(lses[0], lses[1]), lses[2])
    es = [jnp.exp(l - lmax) for l in lses]
    inv = 1.0 / (es[0] + es[1] + es[2])
    for g in range(N_GROUPS):
        cat_ref[:, g * GROUP_WIDTH:(g + 1) * GROUP_WIDTH] = (outs[g] * (es[g] * inv)).astype(BF16)

    zeros_halo = jnp.zeros((CONV_HALO, CONV_WIDTH), F32)
    left = jnp.where(i > 0, glul_ref[0], zeros_halo)
    right = jnp.where(i < n_i - 1, glur_ref[0], zeros_halo)
    conv_blocks = CONV_WIDTH // LANES
    for cb in range(conv_blocks):
        cols = slice(cb * LANES, (cb + 1) * LANES)
        hpad_ref[cb, 0:CONV_HALO] = left[:, cols]
        hpad_ref[cb, CONV_HALO:CONV_HALO + tm] = glu_ref[0, :, cols]
        hpad_ref[cb, CONV_HALO + tm:] = right[:, cols]
    chunk = 64
    off0 = CONV_HALO - CONV_KERNEL // 2
    for r0 in range(0, tm, chunk):
        accs = []
        for cb in range(conv_blocks):
            cols = slice(cb * LANES, (cb + 1) * LANES)
            a = jnp.zeros((chunk, LANES), F32) + cb_ref[:, cols]
            for j in range(CONV_KERNEL):
                a = a + hpad_ref[cb, r0 + off0 + j: r0 + off0 + j + chunk, :] * cw_ref[j:j + 1, cols]
            accs.append(a)
        acc = jnp.concatenate(accs, axis=1)
        mu = jnp.mean(acc, axis=-1, keepdims=True)
        xc = acc - mu
        var = jnp.mean(xc * xc, axis=-1, keepdims=True)
        y = xc * lax.rsqrt(var + NORM_EPS) * lng_ref[...] + lnb_ref[...]
        cat_ref[r0:r0 + chunk, ATTN_WIDTH:] = (y * jax.nn.sigmoid(y)).astype(BF16)

    h = x_ref[0] + jnp.dot(cat_ref[...], wo_ref[...], preferred_element_type=F32)
    h_ref[0] = h
    xn = h * lax.rsqrt(jnp.mean(h * h, axis=-1, keepdims=True) + NORM_EPS) * g2_ref[...]

    for c in range(ROW_CHUNKS):
        xn_ref[0, pl.ds(c, tm, stride=ROW_CHUNKS), :] = xn[:, c * LANES:(c + 1) * LANES]
    xh = xn.astype(BF16)
    xl = (xn - xh.astype(F32)).astype(BF16)
    rr = jnp.dot(jnp.concatenate([xh, xl], axis=0), wr_ref[...], preferred_element_type=F32)
    lg = (rr[:tm, :LANES] + rr[:tm, LANES:]) + (rr[tm:, :LANES] + rr[tm:, LANES:]) + br_ref[...]

    lane = lax.broadcasted_iota(jnp.int32, (tm, LANES), 1)
    lanef = lane.astype(F32)
    big = jnp.float32(1e9)
    is_g = (lane >= N_EXPERTS) & (lane < N_EXPERTS + N_EXPERT_GROUPS)
    gl = jnp.where(is_g, lg, -jnp.inf)
    gmax = jnp.max(gl, axis=-1, keepdims=True)
    grp = jnp.min(jnp.where(gl == gmax, lanef, big), axis=-1, keepdims=True) - N_EXPERTS
    p_grp = 1.0 / jnp.sum(jnp.where(is_g, jnp.exp(lg - gmax), 0.0), axis=-1, keepdims=True)
    in_grp = (lanef >= grp * EXPERTS_PER_GROUP) & (lanef < (grp + 1) * EXPERTS_PER_GROUP)
    el = jnp.where(in_grp, lg, -jnp.inf)
    v1 = jnp.max(el, axis=-1, keepdims=True)
    i1 = jnp.min(jnp.where(el == v1, lanef, big), axis=-1, keepdims=True)
    el2 = jnp.where(lanef == i1, -jnp.inf, el)
    v2 = jnp.max(el2, axis=-1, keepdims=True)
    i2 = jnp.min(jnp.where(el2 == v2, lanef, big), axis=-1, keepdims=True)
    t = jnp.exp(v2 - v1)
    w1 = p_grp / (1.0 + t)
    w2 = p_grp * t / (1.0 + t)

    hit1 = lanef == i1
    hit2 = lanef == i2
    onehot = jnp.where(hit1 | hit2, 1.0, 0.0)
    tot = carry_ref[...] + jnp.dot(tri_ref[...], onehot.astype(BF16), preferred_element_type=F32)
    r1 = jnp.sum(jnp.where(hit1, tot, 0.0), axis=-1, keepdims=True)
    r2 = jnp.sum(jnp.where(hit2, tot, 0.0), axis=-1, keepdims=True)
    carry_ref[...] = carry_ref[...] + jnp.sum(onehot, axis=0, keepdims=True)
    cnt_ref[...] = carry_ref[...]

    route = jnp.where(lane == 0, i1, 0.0)
    route = jnp.where(lane == 1, i2, route)
    route = jnp.where(lane == 2, r1, route)
    route = jnp.where(lane == 3, r2, route)
    route = jnp.where(lane == 4, w1, route)
    route = jnp.where(lane == 5, w2, route)
    route_ref[0] = route


def _mix_call(x, attn_out, glu, conv_w, conv_b, ln_g, ln_b, w_out_bf, g2, wr, br, tri):
    B, S, D = x.shape
    tm = TOKEN_TILE
    nhalo = tm // CONV_HALO
    last_halo = S // CONV_HALO - 1
    in_specs = [pl.BlockSpec((1, tm, D), lambda b, i: (b, i, 0))]
    args = [x]
    for d, (o, l) in zip(DILATIONS, attn_out):
        spec = pl.BlockSpec((1, d, tm // d, GROUP_WIDTH), lambda b, i: (b, 0, i, 0))
        in_specs += [spec, spec]
        args += [o.reshape(B, d, S // d, GROUP_WIDTH), l.reshape(B, d, S // d, GROUP_WIDTH)]
    in_specs += [
        pl.BlockSpec((1, tm, CONV_WIDTH), lambda b, i: (b, i, 0)),
        pl.BlockSpec((1, CONV_HALO, CONV_WIDTH), lambda b, i: (b, jnp.maximum(i * nhalo - 1, 0), 0)),
        pl.BlockSpec((1, CONV_HALO, CONV_WIDTH),
                     lambda b, i: (b, jnp.minimum((i + 1) * nhalo, last_halo), 0)),
    ]
    args += [glu, glu, glu]
    for a in (conv_w, conv_b, ln_g, ln_b, w_out_bf, g2, wr, br, tri):
        in_specs.append(pl.BlockSpec(a.shape, lambda b, i: (0, 0)))
        args.append(a)
    return pl.pallas_call(
        _mix_kernel,
        grid=(B, S // tm),
        in_specs=in_specs,
        out_specs=[pl.BlockSpec((1, tm, D), lambda b, i: (b, i, 0)),
                   pl.BlockSpec((1, tm * ROW_CHUNKS, LANES), lambda b, i: (b, i, 0)),
                   pl.BlockSpec((1, tm, LANES), lambda b, i: (b, i, 0)),
                   pl.BlockSpec((1, LANES), lambda b, i: (0, 0))],
        out_shape=[jax.ShapeDtypeStruct((B, S, D), F32),
                   jax.ShapeDtypeStruct((B, S * ROW_CHUNKS, LANES), F32),
                   jax.ShapeDtypeStruct((B, S, LANES), F32),
                   jax.ShapeDtypeStruct((1, LANES), F32)],
        scratch_shapes=[pltpu.VMEM((tm, D), BF16),
                        pltpu.VMEM((4, GROUP_WIDTH // LANES, tm, LANES), F32),
                        pltpu.VMEM((CONV_WIDTH // LANES, tm + 2 * CONV_HALO, LANES), F32),
                        pltpu.VMEM((1, LANES), F32)],
        compiler_params=_params(2),
        name="mix_outproj_router",
    )(*args)


def _tile_rows(ref, tile):
    return ref.at[pl.ds(pl.multiple_of(tile * ROW_CHUNKS, ROW_CHUNKS), ROW_CHUNKS), :]


def _dispatch_copy(xn_ref, r, xs_hbm, dest_row, sem):
    return pltpu.make_async_copy(_tile_rows(xn_ref, r), _tile_rows(xs_hbm, dest_row), sem)


def _zero_fill_copy(zero_ref, xs_hbm, end_row, sem):
    start = pl.multiple_of((end_row - MOE_ROWS) * ROW_CHUNKS, ROW_CHUNKS)
    return pltpu.make_async_copy(zero_ref, xs_hbm.at[pl.ds(start, MOE_ROWS * ROW_CHUNKS), :], sem)


def _dispatch_kernel(dest_ref, pend_ref, xn_ref, xs_hbm, zero_ref, sem, zsem):
    rows = xn_ref.shape[0] // ROW_CHUNKS
    base = pl.program_id(0) * rows

    @pl.when(pl.program_id(0) == 0)
    def _():
        zero_ref[...] = jnp.zeros_like(zero_ref)
        n_rows = xs_hbm.shape[0] // ROW_CHUNKS
        for wait in (False, True):
            for e in range(N_EXPERTS):
                prev = pend_ref[e - 1] if e else 0

                @pl.when(pend_ref[e] > prev)
                def _():
                    copy = _zero_fill_copy(zero_ref, xs_hbm, pend_ref[e], zsem)
                    copy.wait() if wait else copy.start()

            for j in range(1, N_EXPERTS + 1):
                end_row = pend_ref[N_EXPERTS - 1] + j * MOE_ROWS

                @pl.when(end_row <= n_rows)
                def _():
                    copy = _zero_fill_copy(zero_ref, xs_hbm, end_row, zsem)
                    copy.wait() if wait else copy.start()

    def issue(r, carry):
        for k in range(2):
            _dispatch_copy(xn_ref, r, xs_hbm, dest_ref[2 * (base + r) + k], sem).start(priority=k)
        return carry

    lax.fori_loop(0, rows, issue, 0, unroll=DMA_UNROLL)

    def drain(r, carry):
        for k in range(2):
            _dispatch_copy(xn_ref, r, xs_hbm, 0, sem).wait()
        return carry

    lax.fori_loop(0, rows, drain, 0, unroll=DMA_UNROLL)


def _dispatch_call(dest, pend, xn_tiles, n_rows):
    rows = DISPATCH_ROWS
    n_tok = xn_tiles.shape[0] // ROW_CHUNKS
    grid_spec = pltpu.PrefetchScalarGridSpec(
        num_scalar_prefetch=2,
        grid=(n_tok // rows,),
        in_specs=[pl.BlockSpec((rows * ROW_CHUNKS, LANES), lambda i, d, pe: (i, 0))],
        out_specs=pl.BlockSpec(memory_space=pl.ANY),
        scratch_shapes=[pltpu.VMEM((MOE_ROWS * ROW_CHUNKS, LANES), F32),
                        pltpu.SemaphoreType.DMA, pltpu.SemaphoreType.DMA],
    )
    return pl.pallas_call(
        _dispatch_kernel,
        grid_spec=grid_spec,
        out_shape=jax.ShapeDtypeStruct((n_rows * ROW_CHUNKS, LANES), F32),
        compiler_params=_params(1),
        name="moe_dispatch",
    )(dest, pend, xn_tiles)


def _expert_kernel(slot_e_ref, par_ref, nu_ref, xs_ref, wg0, wu0, wd0, wg1, wu1, wd1, y_ref, xb_ref):
    del slot_e_ref
    step = pl.program_id(0)
    used = step < nu_ref[0]

    for slot, (wg_ref, wu_ref, wd_ref) in enumerate(((wg0, wu0, wd0), (wg1, wu1, wd1))):
        @pl.when(used & (par_ref[step] == slot))
        def _():
            for c in range(ROW_CHUNKS):
                xb_ref[:, c * LANES:(c + 1) * LANES] = (
                    xs_ref[pl.ds(c, MOE_ROWS, stride=ROW_CHUNKS), :].astype(BF16))
            xb = xb_ref[...]
            hg = jnp.dot(xb, wg_ref[0], preferred_element_type=F32)
            hu = jnp.dot(xb, wu_ref[0], preferred_element_type=F32)
            hh = (hg * jax.nn.sigmoid(hg) * hu).astype(BF16)
            y = jnp.dot(hh, wd_ref[0], preferred_element_type=F32)
            for c in range(ROW_CHUNKS):
                y_ref[pl.ds(c, MOE_ROWS, stride=ROW_CHUNKS), :] = y[:, c * LANES:(c + 1) * LANES]

    @pl.when(jnp.logical_not(used))
    def _():
        y_ref[...] = jnp.zeros_like(y_ref)


def _expert_call(slot_expert, parity, n_used, xs_tiles, wg_bf, wu_bf, wd_bf):
    n_blocks = parity.shape[0]
    D = wg_bf.shape[1]
    tile_rows = MOE_ROWS * ROW_CHUNKS

    def row_block(i, se, par, nu):
        return (jnp.minimum(i, nu[0] - 1), 0)

    def slot_block(slot):
        return lambda i, se, par, nu: (se[2 * i + slot], 0, 0)

    weight_specs = []
    for slot in range(2):
        weight_specs += [pl.BlockSpec((1, D, EXPERT_HIDDEN), slot_block(slot)),
                         pl.BlockSpec((1, D, EXPERT_HIDDEN), slot_block(slot)),
                         pl.BlockSpec((1, EXPERT_HIDDEN, D), slot_block(slot))]
    grid_spec = pltpu.PrefetchScalarGridSpec(
        num_scalar_prefetch=3,
        grid=(n_blocks,),
        in_specs=[pl.BlockSpec((tile_rows, LANES), row_block)] + weight_specs,
        out_specs=pl.BlockSpec((tile_rows, LANES), lambda i, se, par, nu: (i, 0)),
        scratch_shapes=[pltpu.VMEM((MOE_ROWS, D), BF16)],
    )
    return pl.pallas_call(
        _expert_kernel,
        grid_spec=grid_spec,
        out_shape=jax.ShapeDtypeStruct((n_blocks * tile_rows, LANES), F32),
        compiler_params=_params(1),
        name="moe_experts",
    )(slot_expert, parity, n_used, xs_tiles, wg_bf, wu_bf, wd_bf, wg_bf, wu_bf, wd_bf)


def _gather_copy(y_hbm, row, ybuf, slot, k, r, sems):
    return pltpu.make_async_copy(_tile_rows(y_hbm, row), _tile_rows(ybuf.at[slot, k], r), sems.at[slot])


def _combine_kernel(dest_ref, h_ref, route_ref, y_hbm, gf_ref, o_ref, ybuf, ynat, sems):
    rows = h_ref.shape[0]
    step = pl.program_id(0)
    n_steps = pl.num_programs(0)

    def issue(s, slot):
        def body(r, carry):
            for k in range(2):
                _gather_copy(y_hbm, dest_ref[2 * (s * rows + r) + k], ybuf, slot, k, r, sems).start(priority=k)
            return carry
        lax.fori_loop(0, rows, body, 0, unroll=DMA_UNROLL)

    def drain(slot):
        def body(r, carry):
            for k in range(2):
                _gather_copy(y_hbm, 0, ybuf, slot, k, r, sems).wait()
            return carry
        lax.fori_loop(0, rows, body, 0, unroll=DMA_UNROLL)

    @pl.when(step == 0)
    def _():
        issue(0, 0)

    for slot in range(2):
        @pl.when((step % 2 == slot) & (step + 1 < n_steps))
        def _():
            issue(step + 1, 1 - slot)

    for slot in range(2):
        @pl.when(step % 2 == slot)
        def _():
            drain(slot)
            for k in range(2):
                for c in range(ROW_CHUNKS):
                    ynat[k, :, c * LANES:(c + 1) * LANES] = ybuf[slot, k, pl.ds(c, rows, stride=ROW_CHUNKS), :]
            route = route_ref[...]
            z = h_ref[...] + (route[:, 4:5] * ynat[0] + route[:, 5:6] * ynat[1])
            o_ref[...] = z * lax.rsqrt(jnp.mean(z * z, axis=-1, keepdims=True) + NORM_EPS) * gf_ref[...]


def _combine_call(dest, h, route, y, gf):
    T, D = h.shape
    rows = COMBINE_ROWS
    grid_spec = pltpu.PrefetchScalarGridSpec(
        num_scalar_prefetch=1,
        grid=(T // rows,),
        in_specs=[
            pl.BlockSpec((rows, D), lambda i, d: (i, 0)),
            pl.BlockSpec((rows, LANES), lambda i, d: (i, 0)),
            pl.BlockSpec(memory_space=pl.ANY),
            pl.BlockSpec((1, D), lambda i, d: (0, 0)),
        ],
        out_specs=pl.BlockSpec((rows, D), lambda i, d: (i, 0)),
        scratch_shapes=[pltpu.VMEM((2, 2, rows * ROW_CHUNKS, LANES), F32), pltpu.VMEM((2, rows, D), F32),
                        pltpu.SemaphoreType.DMA((2,))],
    )
    return pl.pallas_call(
        _combine_kernel,
        grid_spec=grid_spec,
        out_shape=jax.ShapeDtypeStruct((T, D), F32),
        compiler_params=_params(1),
        name="moe_combine_norm",
    )(dest, h, route, y, gf)


def _rope_tables(S):
    half = HEAD_DIM // 2
    inv_freq = ROPE_THETA ** (-jnp.arange(half, dtype=F32) / half)
    ang = jnp.arange(S, dtype=F32)[:, None] * inv_freq[None, :]
    cos = jnp.cos(ang)
    sin = jnp.sin(ang)
    reps = LANES // HEAD_DIM
    return jnp.tile(cos, (1, 2 * reps)), jnp.tile(jnp.concatenate([-sin, sin], axis=1), (1, reps))


def _router_weights(w_rg, b_rg, w_re, b_re):
    D = w_re.shape[0]
    w = jnp.zeros((D, LANES), F32).at[:, :N_EXPERTS].set(w_re)
    w = w.at[:, N_EXPERTS:N_EXPERTS + N_EXPERT_GROUPS].set(w_rg)
    hi = w.astype(BF16)
    lo = (w - hi.astype(F32)).astype(BF16)
    b = jnp.zeros((1, LANES), F32).at[0, :N_EXPERTS].set(b_re)
    b = b.at[0, N_EXPERTS:N_EXPERTS + N_EXPERT_GROUPS].set(b_rg)
    return jnp.concatenate([hi, lo], axis=1), b


def _trunk(x, p):
    B, S, D = x.shape
    T = B * S
    o0, o1, o2, glu = _inproj_call(x, p["g1"], p["w_in"], p["cos"], p["sin"])
    attn_out = []
    for d, qkv in zip(DILATIONS, (o0, o1, o2)):
        attn_out.append(_attn_call(qkv.reshape(3, B * d, S // d, GROUP_WIDTH)))
    h, xn, route, counts = _mix_call(x, attn_out, glu, p["conv_w"], p["conv_b"], p["ln_g"], p["ln_b"],
                                     p["w_out"], p["g2"], p["wr"], p["br"], p["tri"])
    h = h.reshape(T, D)
    xn = xn.reshape(T * ROW_CHUNKS, LANES)
    route = route.reshape(T, LANES)

    eid = route[:, 0:2].astype(jnp.int32)
    rank = route[:, 2:4].astype(jnp.int32)
    cnt = counts[0, :N_EXPERTS].astype(jnp.int32)
    pcnt = (cnt + MOE_ROWS - 1) // MOE_ROWS * MOE_ROWS
    pend = jnp.cumsum(pcnt)
    pstart = pend - pcnt
    experts = jnp.arange(N_EXPERTS, dtype=jnp.int32)
    dest = (jnp.sum(jnp.where(eid[:, :, None] == experts, pstart, 0), axis=-1) + rank).reshape(2 * T)
    n_blocks = (2 * T) // MOE_ROWS + N_EXPERTS
    n_used = pend[-1:] // MOE_ROWS
    block_start = jnp.minimum(jnp.arange(n_blocks, dtype=jnp.int32), n_used - 1) * MOE_ROWS
    block_expert = jnp.sum(block_start[:, None] >= pend[None, :], axis=-1).astype(jnp.int32)
    in_use = (pcnt > 0)[None, :]
    parity = (jnp.sum(in_use & (experts[None, :] < block_expert[:, None]), axis=-1) % 2).astype(jnp.int32)
    later = jnp.where(in_use & (experts[None, :] > block_expert[:, None]), experts[None, :], N_EXPERTS)
    next_expert = jnp.min(later, axis=-1)
    next_expert = jnp.where(next_expert == N_EXPERTS, block_expert, next_expert).astype(jnp.int32)
    slot_expert = jnp.stack([jnp.where(parity == 0, block_expert, next_expert),
                             jnp.where(parity == 1, block_expert, next_expert)], axis=1).reshape(-1)

    xs = _dispatch_call(dest, pend.astype(jnp.int32), xn, n_blocks * MOE_ROWS)
    y = _expert_call(slot_expert, parity, n_used.astype(jnp.int32), xs, p["wg"], p["wu"], p["wd"])
    out = _combine_call(dest, h, route, y, p["gf"])
    return out.reshape(B, S, D)


def kernel(x_prompt, x_sample, norm1_g, w_in, conv_dw_w, conv_dw_b, conv_ln_g, conv_ln_b, w_out, norm2_g,
           router_w_group, router_b_group, router_w_expert, router_b_expert, expert_w_gate, expert_w_up,
           expert_w_down, norm_f_g):
    assert norm1_g.shape[0] == 1, "one encoder layer"
    wr, br = _router_weights(router_w_group[0], router_b_group[0], router_w_expert[0], router_b_expert[0])
    tm = TOKEN_TILE
    tri = (lax.broadcasted_iota(jnp.int32, (tm, tm), 1)
           < lax.broadcasted_iota(jnp.int32, (tm, tm), 0)).astype(BF16)
    cos_t, sin_t = _rope_tables(max(x_prompt.shape[1], x_sample.shape[1]))
    p = {
        "cos": cos_t,
        "sin": sin_t,
        "g1": norm1_g[0][None, :],
        "w_in": w_in[0].astype(BF16),
        "conv_w": jnp.pad(conv_dw_w[0], ((0, 1), (0, 0))),
        "conv_b": conv_dw_b[0][None, :],
        "ln_g": conv_ln_g[0][None, :],
        "ln_b": conv_ln_b[0][None, :],
        "w_out": w_out[0].astype(BF16),
        "g2": norm2_g[0][None, :],
        "wr": wr,
        "br": br,
        "tri": tri,
        "wg": expert_w_gate[0].astype(BF16),
        "wu": expert_w_up[0].astype(BF16),
        "wd": expert_w_down[0].astype(BF16),
        "gf": norm_f_g[None, :],
    }
    return (_trunk(x_prompt, p), _trunk(x_sample, p))
```

```python
import functools

import jax
import jax.numpy as jnp
from jax import lax
from jax.experimental import pallas as pl
from jax.experimental.pallas import tpu as pltpu

D_MODEL = 1024
HEAD_DIM = 64
HEADS_PER_GROUP = 4
GROUP_WIDTH = HEADS_PER_GROUP * HEAD_DIM
DILATIONS = (1, 4, 16)
HALF_WINDOW = 64
N_GROUPS = len(DILATIONS)
ATTN_WIDTH = N_GROUPS * GROUP_WIDTH
CONV_WIDTH = D_MODEL - ATTN_WIDTH
CONV_KERNEL = 31
CONV_HALO = 16
ROPE_THETA = 10000.0
N_EXPERT_GROUPS = 4
EXPERTS_PER_GROUP = 8
N_EXPERTS = N_EXPERT_GROUPS * EXPERTS_PER_GROUP
EXPERT_HIDDEN = D_MODEL // 2
NORM_EPS = 1e-6
NEG_INF = -1e30

LANES = 128
TOKEN_TILE = 512
ATTN_QBLOCK = 128
ATTN_KBLOCK = ATTN_QBLOCK + 2 * HALF_WINDOW
MOE_ROWS = 256
DISPATCH_ROWS = 1024
COMBINE_ROWS = 512
DMA_UNROLL = 8
ROUTE_FIELDS = 8
ROW_CHUNKS = D_MODEL // LANES
VMEM_LIMIT = 56 * 1024 * 1024

F32 = jnp.float32
BF16 = jnp.bfloat16


def _params(n_axes):
    return pltpu.CompilerParams(dimension_semantics=("arbitrary",) * n_axes,
                                vmem_limit_bytes=VMEM_LIMIT)


def _inproj_kernel(x_ref, g_ref, w_ref, cos_ref, sin_ref, o0_ref, o1_ref, o2_ref, glu_ref, scr_ref):
    tm = x_ref.shape[1]
    x = x_ref[0]
    u = x * lax.rsqrt(jnp.mean(x * x, axis=-1, keepdims=True) + NORM_EPS) * g_ref[...]
    ub = u.astype(BF16)
    cos = cos_ref[...]
    sin = sin_ref[...]
    lane = lax.broadcasted_iota(jnp.int32, (tm, LANES), 1)
    first_half = (lane % HEAD_DIM) < (HEAD_DIM // 2)

    def rope(t):
        partner = jnp.where(first_half, pltpu.roll(t, LANES - HEAD_DIM // 2, axis=1),
                            pltpu.roll(t, HEAD_DIM // 2, axis=1))
        return t * cos + partner * sin

    for part in range(3):
        p = jnp.dot(ub, w_ref[:, part * ATTN_WIDTH:(part + 1) * ATTN_WIDTH],
                    preferred_element_type=F32)
        for cb in range(ATTN_WIDTH // LANES):
            t = p[:, cb * LANES:(cb + 1) * LANES]
            if part == 0:
                t = rope(t) * (HEAD_DIM ** -0.5)
            elif part == 1:
                t = rope(t)
            scr_ref[part * (ATTN_WIDTH // LANES) + cb] = t
    c = jnp.dot(ub, w_ref[:, 3 * ATTN_WIDTH:], preferred_element_type=F32)
    glu_ref[0] = c[:, :CONV_WIDTH] * jax.nn.sigmoid(c[:, CONV_WIDTH:])

    lane_blocks = GROUP_WIDTH // LANES
    for part in range(3):
        for g, (d, o_ref) in enumerate(zip(DILATIONS, (o0_ref, o1_ref, o2_ref))):
            for lb in range(lane_blocks):
                cb = part * (ATTN_WIDTH // LANES) + g * lane_blocks + lb
                for c_res in range(d):
                    if d == 1:
                        rows = scr_ref[cb]
                    else:
                        rows = scr_ref[cb, pl.ds(c_res, tm // d, stride=d), :]
                    o_ref[part, 0, c_res, :, lb * LANES:(lb + 1) * LANES] = rows.astype(BF16)


def _inproj_call(x, g1, w_in_bf, cos_t, sin_t):
    B, S, D = x.shape
    tm = TOKEN_TILE
    grid = (B, S // tm)
    out_shape = [jax.ShapeDtypeStruct((3, B, d, S // d, GROUP_WIDTH), BF16) for d in DILATIONS]
    out_shape.append(jax.ShapeDtypeStruct((B, S, CONV_WIDTH), F32))
    out_specs = [pl.BlockSpec((3, 1, d, tm // d, GROUP_WIDTH), lambda b, i: (0, b, 0, i, 0))
                 for d in DILATIONS]
    out_specs.append(pl.BlockSpec((1, tm, CONV_WIDTH), lambda b, i: (b, i, 0)))
    return pl.pallas_call(
        _inproj_kernel,
        grid=grid,
        in_specs=[
            pl.BlockSpec((1, tm, D), lambda b, i: (b, i, 0)),
            pl.BlockSpec((1, D), lambda b, i: (0, 0)),
            pl.BlockSpec(w_in_bf.shape, lambda b, i: (0, 0)),
            pl.BlockSpec((tm, LANES), lambda b, i: (i, 0)),
            pl.BlockSpec((tm, LANES), lambda b, i: (i, 0)),
        ],
        out_specs=out_specs,
        out_shape=out_shape,
        scratch_shapes=[pltpu.VMEM((3 * ATTN_WIDTH // LANES, tm, LANES), F32)],
        compiler_params=_params(2),
        name="inproj",
    )(x, g1, w_in_bf, cos_t, sin_t)


def _attn_kernel(q_ref, k_ref, kl_ref, kr_ref, v_ref, vl_ref, vr_ref, o_ref, lse_ref, kx_ref, vx_ref,
                 *, seq_len):
    bq = q_ref.shape[2]
    hw = HALF_WINDOW
    q0 = pl.program_id(1) * bq
    kx_ref[0:hw] = kl_ref[0, 0]
    kx_ref[hw:hw + bq] = k_ref[0, 0]
    kx_ref[hw + bq:] = kr_ref[0, 0]
    vx_ref[0:hw] = vl_ref[0, 0]
    vx_ref[hw:hw + bq] = v_ref[0, 0]
    vx_ref[hw + bq:] = vr_ref[0, 0]

    nh = HEADS_PER_GROUP
    head_of_lane = lax.broadcasted_iota(jnp.int32, (ATTN_QBLOCK, GROUP_WIDTH), 1) // HEAD_DIM
    qq = lax.broadcasted_iota(jnp.int32, (ATTN_QBLOCK, ATTN_KBLOCK), 0)
    kk = lax.broadcasted_iota(jnp.int32, (ATTN_QBLOCK, ATTN_KBLOCK), 1)
    band = jnp.abs(kk - hw - qq) <= hw

    for j in range(bq // ATTN_QBLOCK):
        r0 = j * ATTN_QBLOCK
        qj = q_ref[0, 0, r0:r0 + ATTN_QBLOCK, :]
        kj = kx_ref[r0:r0 + ATTN_KBLOCK, :]
        vj = vx_ref[r0:r0 + ATTN_KBLOCK, :]
        kpos = kk + (q0 + r0 - hw)
        valid = band & (kpos >= 0) & (kpos < seq_len)
        qs = jnp.concatenate([jnp.where(head_of_lane == h, qj, jnp.zeros_like(qj)) for h in range(nh)],
                             axis=0)
        s = lax.dot_general(qs, kj, (((1,), (1,)), ((), ())), preferred_element_type=F32)
        s = jnp.where(jnp.concatenate([valid] * nh, axis=0), s, NEG_INF)
        m = jnp.max(s, axis=-1, keepdims=True)
        p = jnp.exp(s - m)
        den = jnp.sum(p, axis=-1, keepdims=True)
        r = jnp.dot(p.astype(BF16), vj, preferred_element_type=F32) / den
        lse = m + jnp.log(den)
        out = jnp.zeros((ATTN_QBLOCK, GROUP_WIDTH), F32)
        lse_full = jnp.zeros((ATTN_QBLOCK, GROUP_WIDTH), F32)
        for h in range(nh):
            sel = head_of_lane == h
            out = jnp.where(sel, r[h * ATTN_QBLOCK:(h + 1) * ATTN_QBLOCK], out)
            lse_full = jnp.where(sel, lse[h * ATTN_QBLOCK:(h + 1) * ATTN_QBLOCK], lse_full)
        o_ref[0, r0:r0 + ATTN_QBLOCK, :] = out.astype(BF16)
        lse_ref[0, r0:r0 + ATTN_QBLOCK, :] = lse_full


def _attn_call(qkv):
    _, nseq, L, W = qkv.shape
    bq = min(L, TOKEN_TILE)
    hw = HALF_WINDOW
    nhb = bq // hw
    last_hb = L // hw - 1

    def main(part):
        return pl.BlockSpec((1, 1, bq, W), lambda n, i: (part, n, i, 0))

    def left(part):
        return pl.BlockSpec((1, 1, hw, W), lambda n, i: (part, n, jnp.maximum(i * nhb - 1, 0), 0))

    def right(part):
        return pl.BlockSpec((1, 1, hw, W), lambda n, i: (part, n, jnp.minimum((i + 1) * nhb, last_hb), 0))

    return pl.pallas_call(
        functools.partial(_attn_kernel, seq_len=L),
        grid=(nseq, L // bq),
        in_specs=[main(0), main(1), left(1), right(1), main(2), left(2), right(2)],
        out_specs=[pl.BlockSpec((1, bq, W), lambda n, i: (n, i, 0)),
                   pl.BlockSpec((1, bq, W), lambda n, i: (n, i, 0))],
        out_shape=[jax.ShapeDtypeStruct((nseq, L, W), BF16),
                   jax.ShapeDtypeStruct((nseq, L, W), F32)],
        scratch_shapes=[pltpu.VMEM((bq + 2 * hw, W), BF16), pltpu.VMEM((bq + 2 * hw, W), BF16)],
        compiler_params=_params(2),
        name="band_attn",
    )(qkv, qkv, qkv, qkv, qkv, qkv, qkv)


def _mix_kernel(x_ref, o0_ref, l0_ref, o1_ref, l1_ref, o2_ref, l2_ref, glu_ref, glul_ref, glur_ref,
                cw_ref, cb_ref, lng_ref, lnb_ref, wo_ref, g2_ref, wr_ref, br_ref, tri_ref,
                h_ref, xn_ref, route_ref, routet_ref, cnt_ref,
                cat_ref, nat_ref, hpad_ref, carry_ref):
    tm = x_ref.shape[1]
    b = pl.program_id(0)
    i = pl.program_id(1)
    n_i = pl.num_programs(1)

    @pl.when((b == 0) & (i == 0))
    def _():
        carry_ref[...] = jnp.zeros_like(carry_ref)

    for slot, (src, d) in enumerate(((o1_ref, DILATIONS[1]), (l1_ref, DILATIONS[1]),
                                     (o2_ref, DILATIONS[2]), (l2_ref, DILATIONS[2]))):
        for c_res in range(d):
            for lb in range(GROUP_WIDTH // LANES):
                nat_ref[slot, lb, pl.ds(c_res, tm // d, stride=d), :] = (
                    src[0, c_res, :, lb * LANES:(lb + 1) * LANES].astype(F32))

    def natural(slot):
        return jnp.concatenate([nat_ref[slot, lb] for lb in range(GROUP_WIDTH // LANES)], axis=1)

    outs = (o0_ref[0, 0].astype(F32), natural(0), natural(2))
    lses = (l0_ref[0, 0], natural(1), natural(3))
    lmax = jnp.maximum(jnp.maximum(lses[0], lses[1]), lses[2])
    es = [jnp.exp(l - lmax) for l in lses]
    inv = 1.0 / (es[0] + es[1] + es[2])
    for g in range(N_GROUPS):
        cat_ref[:, g * GROUP_WIDTH:(g + 1) * GROUP_WIDTH] = (outs[g] * (es[g] * inv)).astype(BF16)

    zeros_halo = jnp.zeros((CONV_HALO, CONV_WIDTH), F32)
    left = jnp.where(i > 0, glul_ref[0], zeros_halo)
    right = jnp.where(i < n_i - 1, glur_ref[0], zeros_halo)
    conv_blocks = CONV_WIDTH // LANES
    for cb in range(conv_blocks):
        cols = slice(cb * LANES, (cb + 1) * LANES)
        hpad_ref[cb, 0:CONV_HALO] = left[:, cols]
        hpad_ref[cb, CONV_HALO:CONV_HALO + tm] = glu_ref[0, :, cols]
        hpad_ref[cb, CONV_HALO + tm:] = right[:, cols]
    chunk = 64
    off0 = CONV_HALO - CONV_KERNEL // 2
    for r0 in range(0, tm, chunk):
        accs = []
        for cb in range(conv_blocks):
            cols = slice(cb * LANES, (cb + 1) * LANES)
            a = jnp.zeros((chunk, LANES), F32) + cb_ref[:, cols]
            for j in range(CONV_KERNEL):
                a = a + hpad_ref[cb, r0 + off0 + j: r0 + off0 + j + chunk, :] * cw_ref[j:j + 1, cols]
            accs.append(a)
        acc = jnp.concatenate(accs, axis=1)
        mu = jnp.mean(acc, axis=-1, keepdims=True)
        xc = acc - mu
        var = jnp.mean(xc * xc, axis=-1, keepdims=True)
        y = xc * lax.rsqrt(var + NORM_EPS) * lng_ref[...] + lnb_ref[...]
        cat_ref[r0:r0 + chunk, ATTN_WIDTH:] = (y * jax.nn.sigmoid(y)).astype(BF16)

    h = x_ref[0] + jnp.dot(cat_ref[...], wo_ref[...], preferred_element_type=F32)
    h_ref[0] = h
    xn = h * lax.rsqrt(jnp.mean(h * h, axis=-1, keepdims=True) + NORM_EPS) * g2_ref[...]

    for c in range(ROW_CHUNKS):
        xn_ref[0, pl.ds(c, tm, stride=ROW_CHUNKS), :] = xn[:, c * LANES:(c + 1) * LANES]

    xh = xn.astype(BF16)
    xl = (xn - xh.astype(F32)).astype(BF16)
    rr = jnp.dot(jnp.concatenate([xh, xl], axis=0), wr_ref[...], preferred_element_type=F32)
    lg = (rr[:tm, :LANES] + rr[:tm, LANES:]) + (rr[tm:, :LANES] + rr[tm:, LANES:]) + br_ref[...]

    lane = lax.broadcasted_iota(jnp.int32, (tm, LANES), 1)
    lanef = lane.astype(F32)
    big = jnp.float32(1e9)
    is_g = (lane >= N_EXPERTS) & (lane < N_EXPERTS + N_EXPERT_GROUPS)
    gl = jnp.where(is_g, lg, -jnp.inf)
    gmax = jnp.max(gl, axis=-1, keepdims=True)
    grp = jnp.min(jnp.where(gl == gmax, lanef, big), axis=-1, keepdims=True) - N_EXPERTS
    p_grp = 1.0 / jnp.sum(jnp.where(is_g, jnp.exp(lg - gmax), 0.0), axis=-1, keepdims=True)
    in_grp = (lanef >= grp * EXPERTS_PER_GROUP) & (lanef < (grp + 1) * EXPERTS_PER_GROUP)
    el = jnp.where(in_grp, lg, -jnp.inf)
    v1 = jnp.max(el, axis=-1, keepdims=True)
    i1 = jnp.min(jnp.where(el == v1, lanef, big), axis=-1, keepdims=True)
    el2 = jnp.where(lanef == i1, -jnp.inf, el)
    v2 = jnp.max(el2, axis=-1, keepdims=True)
    i2 = jnp.min(jnp.where(el2 == v2, lanef, big), axis=-1, keepdims=True)
    t = jnp.exp(v2 - v1)
    w1 = p_grp / (1.0 + t)
    w2 = p_grp * t / (1.0 + t)

    hit1 = lanef == i1
    hit2 = lanef == i2
    onehot = jnp.where(hit1 | hit2, 1.0, 0.0)
    tot = carry_ref[...] + jnp.dot(tri_ref[...], onehot.astype(BF16), preferred_element_type=F32)
    r1 = jnp.sum(jnp.where(hit1, tot, 0.0), axis=-1, keepdims=True)
    r2 = jnp.sum(jnp.where(hit2, tot, 0.0), axis=-1, keepdims=True)
    carry_ref[...] = carry_ref[...] + jnp.sum(onehot, axis=0, keepdims=True)
    cnt_ref[...] = carry_ref[...]

    route = jnp.where(lane == 0, i1, 0.0)
    route = jnp.where(lane == 1, i2, route)
    route = jnp.where(lane == 2, r1, route)
    route = jnp.where(lane == 3, r2, route)
    route = jnp.where(lane == 4, w1, route)
    route = jnp.where(lane == 5, w2, route)
    route_ref[0] = route
    routet_ref[...] = route.T[:ROUTE_FIELDS]


def _mix_call(x, attn_out, glu, conv_w, conv_b, ln_g, ln_b, w_out_bf, g2, wr, br, tri):
    B, S, D = x.shape
    tm = TOKEN_TILE
    nhalo = tm // CONV_HALO
    last_halo = S // CONV_HALO - 1
    in_specs = [pl.BlockSpec((1, tm, D), lambda b, i: (b, i, 0))]
    args = [x]
    for d, (o, l) in zip(DILATIONS, attn_out):
        spec = pl.BlockSpec((1, d, tm // d, GROUP_WIDTH), lambda b, i: (b, 0, i, 0))
        in_specs += [spec, spec]
        args += [o.reshape(B, d, S // d, GROUP_WIDTH), l.reshape(B, d, S // d, GROUP_WIDTH)]
    in_specs += [
        pl.BlockSpec((1, tm, CONV_WIDTH), lambda b, i: (b, i, 0)),
        pl.BlockSpec((1, CONV_HALO, CONV_WIDTH), lambda b, i: (b, jnp.maximum(i * nhalo - 1, 0), 0)),
        pl.BlockSpec((1, CONV_HALO, CONV_WIDTH),
                     lambda b, i: (b, jnp.minimum((i + 1) * nhalo, last_halo), 0)),
    ]
    args += [glu, glu, glu]
    for a in (conv_w, conv_b, ln_g, ln_b, w_out_bf, g2, wr, br, tri):
        in_specs.append(pl.BlockSpec(a.shape, lambda b, i: (0, 0)))
        args.append(a)
    return pl.pallas_call(
        _mix_kernel,
        grid=(B, S // tm),
        in_specs=in_specs,
        out_specs=[pl.BlockSpec((1, tm, D), lambda b, i: (b, i, 0)),
                   pl.BlockSpec((1, tm * ROW_CHUNKS, LANES), lambda b, i: (b, i, 0)),
                   pl.BlockSpec((1, tm, LANES), lambda b, i: (b, i, 0)),
                   pl.BlockSpec((ROUTE_FIELDS, tm), lambda b, i: (0, b * (S // tm) + i)),
                   pl.BlockSpec((1, LANES), lambda b, i: (0, 0))],
        out_shape=[jax.ShapeDtypeStruct((B, S, D), F32),
                   jax.ShapeDtypeStruct((B, S * ROW_CHUNKS, LANES), F32),
                   jax.ShapeDtypeStruct((B, S, LANES), F32),
                   jax.ShapeDtypeStruct((ROUTE_FIELDS, B * S), F32),
                   jax.ShapeDtypeStruct((1, LANES), F32)],
        scratch_shapes=[pltpu.VMEM((tm, D), BF16),
                        pltpu.VMEM((4, GROUP_WIDTH // LANES, tm, LANES), F32),
                        pltpu.VMEM((CONV_WIDTH // LANES, tm + 2 * CONV_HALO, LANES), F32),
                        pltpu.VMEM((1, LANES), F32)],
        compiler_params=_params(2),
        name="mix_outproj_router",
    )(*args)


def _tile_rows(ref, tile):
    return ref.at[pl.ds(pl.multiple_of(tile * ROW_CHUNKS, ROW_CHUNKS), ROW_CHUNKS), :]


def _dispatch_copy(xn_ref, r, xs_hbm, dest_row, sem):
    return pltpu.make_async_copy(_tile_rows(xn_ref, r), _tile_rows(xs_hbm, dest_row), sem)


def _zero_fill_copy(zero_ref, xs_hbm, end_row, sem):
    start = pl.multiple_of((end_row - MOE_ROWS) * ROW_CHUNKS, ROW_CHUNKS)
    return pltpu.make_async_copy(zero_ref, xs_hbm.at[pl.ds(start, MOE_ROWS * ROW_CHUNKS), :], sem)


def _dispatch_kernel(dest_ref, pend_ref, xn_ref, xs_hbm, zero_ref, sem, zsem):
    rows = xn_ref.shape[0] // ROW_CHUNKS
    base = pl.program_id(0) * rows
    n_tok = pl.num_programs(0) * rows

    @pl.when(pl.program_id(0) == 0)
    def _():
        zero_ref[...] = jnp.zeros_like(zero_ref)
        n_rows = xs_hbm.shape[0] // ROW_CHUNKS
        for wait in (False, True):
            for e in range(N_EXPERTS):
                prev = pend_ref[e - 1] if e else 0

                @pl.when(pend_ref[e] > prev)
                def _():
                    copy = _zero_fill_copy(zero_ref, xs_hbm, pend_ref[e], zsem)
                    copy.wait() if wait else copy.start()

            for j in range(1, N_EXPERTS + 1):
                end_row = pend_ref[N_EXPERTS - 1] + j * MOE_ROWS

                @pl.when(end_row <= n_rows)
                def _():
                    copy = _zero_fill_copy(zero_ref, xs_hbm, end_row, zsem)
                    copy.wait() if wait else copy.start()

    def issue(r, carry):
        for k in range(2):
            _dispatch_copy(xn_ref, r, xs_hbm, dest_ref[k * n_tok + base + r], sem).start(priority=k)
        return carry

    lax.fori_loop(0, rows, issue, 0, unroll=DMA_UNROLL)

    def drain(r, carry):
        for k in range(2):
            _dispatch_copy(xn_ref, r, xs_hbm, 0, sem).wait()
        return carry

    lax.fori_loop(0, rows, drain, 0, unroll=DMA_UNROLL)


def _dispatch_call(dest, pend, xn_tiles, n_rows):
    rows = DISPATCH_ROWS
    n_tok = xn_tiles.shape[0] // ROW_CHUNKS
    grid_spec = pltpu.PrefetchScalarGridSpec(
        num_scalar_prefetch=2,
        grid=(n_tok // rows,),
        in_specs=[pl.BlockSpec((rows * ROW_CHUNKS, LANES), lambda i, d, pe: (i, 0))],
        out_specs=pl.BlockSpec(memory_space=pl.ANY),
        scratch_shapes=[pltpu.VMEM((MOE_ROWS * ROW_CHUNKS, LANES), F32),
                        pltpu.SemaphoreType.DMA, pltpu.SemaphoreType.DMA],
    )
    return pl.pallas_call(
        _dispatch_kernel,
        grid_spec=grid_spec,
        out_shape=jax.ShapeDtypeStruct((n_rows * ROW_CHUNKS, LANES), F32),
        compiler_params=_params(1),
        name="moe_dispatch",
    )(dest, pend, xn_tiles)


def _expert_kernel(slot_e_ref, par_ref, nu_ref, xs_ref, wg0, wu0, wd0, wg1, wu1, wd1, y_ref, xb_ref):
    del slot_e_ref
    step = pl.program_id(0)
    used = step < nu_ref[0]

    for slot, (wg_ref, wu_ref, wd_ref) in enumerate(((wg0, wu0, wd0), (wg1, wu1, wd1))):
        @pl.when(used & (par_ref[step] == slot))
        def _():
            for c in range(ROW_CHUNKS):
                xb_ref[:, c * LANES:(c + 1) * LANES] = (
                    xs_ref[pl.ds(c, MOE_ROWS, stride=ROW_CHUNKS), :].astype(BF16))
            xb = xb_ref[...]
            hg = jnp.dot(xb, wg_ref[0], preferred_element_type=F32)
            hu = jnp.dot(xb, wu_ref[0], preferred_element_type=F32)
            hh = (hg * jax.nn.sigmoid(hg) * hu).astype(BF16)
            y = jnp.dot(hh, wd_ref[0], preferred_element_type=F32)
            for c in range(ROW_CHUNKS):
                y_ref[pl.ds(c, MOE_ROWS, stride=ROW_CHUNKS), :] = y[:, c * LANES:(c + 1) * LANES]

    @pl.when(jnp.logical_not(used))
    def _():
        y_ref[...] = jnp.zeros_like(y_ref)


def _expert_call(slot_expert, parity, n_used, xs_tiles, wg_bf, wu_bf, wd_bf):
    n_blocks = parity.shape[0]
    D = wg_bf.shape[1]
    tile_rows = MOE_ROWS * ROW_CHUNKS

    def row_block(i, se, par, nu):
        return (jnp.minimum(i, nu[0] - 1), 0)

    def slot_block(slot):
        return lambda i, se, par, nu: (se[2 * i + slot], 0, 0)

    weight_specs = []
    for slot in range(2):
        weight_specs += [pl.BlockSpec((1, D, EXPERT_HIDDEN), slot_block(slot)),
                         pl.BlockSpec((1, D, EXPERT_HIDDEN), slot_block(slot)),
                         pl.BlockSpec((1, EXPERT_HIDDEN, D), slot_block(slot))]
    grid_spec = pltpu.PrefetchScalarGridSpec(
        num_scalar_prefetch=3,
        grid=(n_blocks,),
        in_specs=[pl.BlockSpec((tile_rows, LANES), row_block)] + weight_specs,
        out_specs=pl.BlockSpec((tile_rows, LANES), lambda i, se, par, nu: (i, 0)),
        scratch_shapes=[pltpu.VMEM((MOE_ROWS, D), BF16)],
    )
    return pl.pallas_call(
        _expert_kernel,
        grid_spec=grid_spec,
        out_shape=jax.ShapeDtypeStruct((n_blocks * tile_rows, LANES), F32),
        compiler_params=_params(1),
        name="moe_experts",
    )(slot_expert, parity, n_used, xs_tiles, wg_bf, wu_bf, wd_bf, wg_bf, wu_bf, wd_bf)


def _gather_copy(y_hbm, row, ybuf, slot, k, r, sems):
    return pltpu.make_async_copy(_tile_rows(y_hbm, row), _tile_rows(ybuf.at[slot, k], r), sems.at[slot])


def _combine_kernel(dest_ref, h_ref, route_ref, y_hbm, gf_ref, o_ref, ybuf, ynat, sems):
    rows = h_ref.shape[0]
    step = pl.program_id(0)
    n_steps = pl.num_programs(0)
    n_tok = n_steps * rows

    def issue(s, slot):
        def body(r, carry):
            for k in range(2):
                _gather_copy(y_hbm, dest_ref[k * n_tok + s * rows + r], ybuf, slot, k, r, sems).start(priority=k)
            return carry
        lax.fori_loop(0, rows, body, 0, unroll=DMA_UNROLL)

    def drain(slot):
        def body(r, carry):
            for k in range(2):
                _gather_copy(y_hbm, 0, ybuf, slot, k, r, sems).wait()
            return carry
        lax.fori_loop(0, rows, body, 0, unroll=DMA_UNROLL)

    @pl.when(step == 0)
    def _():
        issue(0, 0)

    for slot in range(2):
        @pl.when((step % 2 == slot) & (step + 1 < n_steps))
        def _():
            issue(step + 1, 1 - slot)

    for slot in range(2):
        @pl.when(step % 2 == slot)
        def _():
            drain(slot)
            for k in range(2):
                for c in range(ROW_CHUNKS):
                    ynat[k, :, c * LANES:(c + 1) * LANES] = ybuf[slot, k, pl.ds(c, rows, stride=ROW_CHUNKS), :]
            route = route_ref[...]
            z = h_ref[...] + (route[:, 4:5] * ynat[0] + route[:, 5:6] * ynat[1])
            o_ref[...] = z * lax.rsqrt(jnp.mean(z * z, axis=-1, keepdims=True) + NORM_EPS) * gf_ref[...]


def _combine_call(dest, h, route, y, gf):
    T, D = h.shape
    rows = COMBINE_ROWS
    grid_spec = pltpu.PrefetchScalarGridSpec(
        num_scalar_prefetch=1,
        grid=(T // rows,),
        in_specs=[
            pl.BlockSpec((rows, D), lambda i, d: (i, 0)),
            pl.BlockSpec((rows, LANES), lambda i, d: (i, 0)),
            pl.BlockSpec(memory_space=pl.ANY),
            pl.BlockSpec((1, D), lambda i, d: (0, 0)),
        ],
        out_specs=pl.BlockSpec((rows, D), lambda i, d: (i, 0)),
        scratch_shapes=[pltpu.VMEM((2, 2, rows * ROW_CHUNKS, LANES), F32), pltpu.VMEM((2, rows, D), F32),
                        pltpu.SemaphoreType.DMA((2,))],
    )
    return pl.pallas_call(
        _combine_kernel,
        grid_spec=grid_spec,
        out_shape=jax.ShapeDtypeStruct((T, D), F32),
        compiler_params=_params(1),
        name="moe_combine_norm",
    )(dest, h, route, y, gf)


def _rope_tables(S):
    half = HEAD_DIM // 2
    inv_freq = ROPE_THETA ** (-jnp.arange(half, dtype=F32) / half)
    ang = jnp.arange(S, dtype=F32)[:, None] * inv_freq[None, :]
    cos = jnp.cos(ang)
    sin = jnp.sin(ang)
    reps = LANES // HEAD_DIM
    return jnp.tile(cos, (1, 2 * reps)), jnp.tile(jnp.concatenate([-sin, sin], axis=1), (1, reps))


def _router_weights(w_rg, b_rg, w_re, b_re):
    D = w_re.shape[0]
    w = jnp.zeros((D, LANES), F32).at[:, :N_EXPERTS].set(w_re)
    w = w.at[:, N_EXPERTS:N_EXPERTS + N_EXPERT_GROUPS].set(w_rg)
    hi = w.astype(BF16)
    lo = (w - hi.astype(F32)).astype(BF16)
    b = jnp.zeros((1, LANES), F32).at[0, :N_EXPERTS].set(b_re)
    b = b.at[0, N_EXPERTS:N_EXPERTS + N_EXPERT_GROUPS].set(b_rg)
    return jnp.concatenate([hi, lo], axis=1), b


def _trunk(x, p):
    B, S, D = x.shape
    T = B * S
    o0, o1, o2, glu = _inproj_call(x, p["g1"], p["w_in"], p["cos"], p["sin"])
    attn_out = []
    for d, qkv in zip(DILATIONS, (o0, o1, o2)):
        attn_out.append(_attn_call(qkv.reshape(3, B * d, S // d, GROUP_WIDTH)))
    h, xn, route, route_t, counts = _mix_call(x, attn_out, glu, p["conv_w"], p["conv_b"], p["ln_g"], p["ln_b"],
                                              p["w_out"], p["g2"], p["wr"], p["br"], p["tri"])
    h = h.reshape(T, D)
    xn = xn.reshape(T * ROW_CHUNKS, LANES)
    route = route.reshape(T, LANES)

    eid = route_t[0:2].astype(jnp.int32)
    rank = route_t[2:4].astype(jnp.int32)
    cnt = counts[0, :N_EXPERTS].astype(jnp.int32)
    pcnt = (cnt + MOE_ROWS - 1) // MOE_ROWS * MOE_ROWS
    pend = jnp.cumsum(pcnt)
    pstart = pend - pcnt
    experts = jnp.arange(N_EXPERTS, dtype=jnp.int32)
    dest = rank
    for e in range(N_EXPERTS):
        dest = dest + jnp.where(eid == e, pstart[e], 0)
    dest = dest.reshape(2 * T)
    n_blocks = (2 * T) // MOE_ROWS + N_EXPERTS
    n_used = pend[-1:] // MOE_ROWS
    block_start = jnp.minimum(jnp.arange(n_blocks, dtype=jnp.int32), n_used - 1) * MOE_ROWS
    block_expert = jnp.sum(block_start[:, None] >= pend[None, :], axis=-1).astype(jnp.int32)
    in_use = (pcnt > 0)[None, :]
    parity = (jnp.sum(in_use & (experts[None, :] < block_expert[:, None]), axis=-1) % 2).astype(jnp.int32)
    later = jnp.where(in_use & (experts[None, :] > block_expert[:, None]), experts[None, :], N_EXPERTS)
    next_expert = jnp.min(later, axis=-1)
    next_expert = jnp.where(next_expert == N_EXPERTS, block_expert, next_expert).astype(jnp.int32)
    slot_expert = jnp.stack([jnp.where(parity == 0, block_expert, next_expert),
                             jnp.where(parity == 1, block_expert, next_expert)], axis=1).reshape(-1)

    xs = _dispatch_call(dest, pend.astype(jnp.int32), xn, n_blocks * MOE_ROWS)
    y = _expert_call(slot_expert, parity, n_used.astype(jnp.int32), xs, p["wg"], p["wu"], p["wd"])
    out = _combine_call(dest, h, route, y, p["gf"])
    return out.reshape(B, S, D)


def kernel(x_prompt, x_sample, norm1_g, w_in, conv_dw_w, conv_dw_b, conv_ln_g, conv_ln_b, w_out, norm2_g,
           router_w_group, router_b_group, router_w_expert, router_b_expert, expert_w_gate, expert_w_up,
           expert_w_down, norm_f_g):
    assert norm1_g.shape[0] == 1, "one encoder layer"
    wr, br = _router_weights(router_w_group[0], router_b_group[0], router_w_expert[0], router_b_expert[0])
    tm = TOKEN_TILE
    tri = (lax.broadcasted_iota(jnp.int32, (tm, tm), 1)
           < lax.broadcasted_iota(jnp.int32, (tm, tm), 0)).astype(BF16)
    cos_t, sin_t = _rope_tables(max(x_prompt.shape[1], x_sample.shape[1]))
    p = {
        "cos": cos_t,
        "sin": sin_t,
        "g1": norm1_g[0][None, :],
        "w_in": w_in[0].astype(BF16),
        "conv_w": jnp.pad(conv_dw_w[0], ((0, 1), (0, 0))),
        "conv_b": conv_dw_b[0][None, :],
        "ln_g": conv_ln_g[0][None, :],
        "ln_b": conv_ln_b[0][None, :],
        "w_out": w_out[0].astype(BF16),
        "g2": norm2_g[0][None, :],
        "wr": wr,
        "br": br,
        "tri": tri,
        "wg": expert_w_gate[0].astype(BF16),
        "wu": expert_w_up[0].astype(BF16),
        "wd": expert_w_down[0].astype(BF16),
        "gf": norm_f_g[None, :],
    }
    return (_trunk(x_prompt, p), _trunk(x_sample, p))
```
